```python
import math
import jax, jax.numpy as jnp
from jax import lax
import numpy as np

D_MODEL = 1024
BATCH = 32
SEQ = 2048
DEPTH = 1

HEAD_DIM = 64
DA_HEADS = 8
DA_V_DIM = 2 * HEAD_DIM
NSA_HEADS = 16
NSA_GROUPS = 2
NSA_HPG = NSA_HEADS // NSA_GROUPS
CMP_BLOCK = 32
CMP_STRIDE = 16
CMP_HIDDEN = 256
SLC_BLOCK = 64
SLC_TOPN = 16
WIN = 512
Q_BLOCK = 128
SLC_Q_BLOCK = 64
D_FF = 2816
ROPE_THETA = 10000.0
EPS = 1e-6
NEG_INF = -1e30
FORCE_SCORE = 1e9

DA_QK_W = DA_HEADS * 2 * HEAD_DIM
DA_V_W = DA_HEADS * DA_V_DIM
NSA_Q_W = NSA_HEADS * HEAD_DIM
NSA_KV_W = NSA_GROUPS * HEAD_DIM
NSA_GATE_W = 3 * NSA_HEADS
IN_WIDTHS = (DA_QK_W, DA_QK_W, DA_V_W, NSA_Q_W) + (NSA_KV_W,) * 6 + (NSA_GATE_W, D_MODEL, D_MODEL)
D_IN = sum(IN_WIDTHS)

kernel_name = 'hybrid_diffattn_nsa_macaron'


def rmsnorm(x, g):
    xf = x.astype(jnp.float32)
    y = xf * lax.rsqrt(jnp.mean(xf * xf, axis=-1, keepdims=True) + EPS)
    return (y * g.astype(jnp.float32)).astype(x.dtype)


def swiglu(h, w1, w3, w2):
    return (jax.nn.silu(h @ w1) * (h @ w3)) @ w2


def rope_tables(T, dim):
    pos = jnp.arange(T, dtype=jnp.float32)
    inv = 1.0 / (ROPE_THETA ** (jnp.arange(0, dim, 2, dtype=jnp.float32) / dim))
    ang = pos[:, None] * inv[None, :]
    return jnp.cos(ang), jnp.sin(ang)


def apply_rope(x, cos, sin):
    half = x.shape[-1] // 2
    shape = (1, x.shape[1]) + (1,) * (x.ndim - 3) + (half,)
    c = cos.reshape(shape).astype(x.dtype)
    s = sin.reshape(shape).astype(x.dtype)
    x1, x2 = x[..., :half], x[..., half:]
    return jnp.concatenate([x1 * c - x2 * s, x2 * c + x1 * s], axis=-1)


def _sweep(block_fn, n_blocks):
    out = lax.map(block_fn, jnp.arange(n_blocks))
    out = jnp.moveaxis(out, 0, 1)
    return out.reshape((out.shape[0], n_blocks * out.shape[2]) + out.shape[3:])


def diff_attention(q, k, v, cos, sin, lam_params, head_gain, lam_init):
    B, T = q.shape[0], q.shape[1]
    q = apply_rope(q, cos, sin)
    k = apply_rope(k, cos, sin)
    lp = lam_params.astype(jnp.float32)
    lam = jnp.exp(jnp.sum(lp[0] * lp[1])) - jnp.exp(jnp.sum(lp[2] * lp[3])) + lam_init
    kpos = jnp.arange(T)
    scale = HEAD_DIM ** -0.5

    def block(i):
        qb = lax.dynamic_slice_in_dim(q, i * Q_BLOCK, Q_BLOCK, axis=1)
        qpos = i * Q_BLOCK + jnp.arange(Q_BLOCK)
        s = jnp.einsum('bqhcd,bkhcd->bhcqk', qb, k).astype(jnp.float32) * scale
        s = jnp.where(kpos[None, :] <= qpos[:, None], s, NEG_INF)
        p = jax.nn.softmax(s, axis=-1)
        a = p[:, :, 0] - lam * p[:, :, 1]
        return jnp.einsum('bhqk,bkhe->bqhe', a.astype(v.dtype), v)

    o = _sweep(block, T // Q_BLOCK)
    o = rmsnorm(o, head_gain) * (1.0 - lam_init)
    return o.reshape(B, T, DA_V_W)


def compress(x_raw, pos, w1, w2):
    B, T = x_raw.shape[0], x_raw.shape[1]
    n_cmp = (T - CMP_BLOCK) // CMP_STRIDE + 1
    idx = np.arange(n_cmp)[:, None] * CMP_STRIDE + np.arange(CMP_BLOCK)[None, :]
    blocks = x_raw[:, idx] + pos[:, None, :]
    blocks = jnp.swapaxes(blocks, 2, 3).reshape(B, n_cmp, NSA_GROUPS, CMP_BLOCK * HEAD_DIM)
    return jax.nn.gelu(blocks @ w1) @ w2


def _overlap_matrix(n_cmp, n_slc):
    start = np.arange(n_cmp) * CMP_STRIDE
    sb = np.arange(n_slc) * SLC_BLOCK
    ov = (start[:, None] < sb[None, :] + SLC_BLOCK) & (start[:, None] + CMP_BLOCK > sb[None, :])
    return ov.astype(np.float32)


def nsa(q, k_cmp, v_cmp, k_slc, v_slc, k_win, v_win, gate_logits, cos, sin, cmp_pos, cmp_w1, cmp_w2):
    B, T = q.shape[0], q.shape[1]
    dt = q.dtype
    scale = HEAD_DIM ** -0.5
    t = jnp.arange(T)

    kc = compress(k_cmp, cmp_pos[0], cmp_w1[0], cmp_w2[0])
    vc = compress(v_cmp, cmp_pos[1], cmp_w1[1], cmp_w2[1])
    n_cmp = kc.shape[1]
    s = jnp.einsum('btgjd,bngd->btgjn', q, kc).astype(jnp.float32) * scale
    c_valid = ((jnp.arange(n_cmp) * CMP_STRIDE + CMP_BLOCK - 1)[None, :] <= t[:, None])[None, :, None, None, :]
    p_cmp = jnp.where(c_valid, jax.nn.softmax(jnp.where(c_valid, s, NEG_INF), axis=-1), 0.0)
    o_cmp = jnp.einsum('btgjn,bngd->btgjd', p_cmp.astype(dt), vc)

    n_slc = T // SLC_BLOCK
    overlap = jnp.asarray(_overlap_matrix(n_cmp, n_slc))
    imp = jnp.einsum('btgjn,ns->btgs', p_cmp, overlap)
    blk_t = t // SLC_BLOCK
    sblk = jnp.arange(n_slc)
    causal = sblk[None, :] <= blk_t[:, None]
    forced = (sblk[None, :] == 0) | (sblk[None, :] == blk_t[:, None]) | (sblk[None, :] == blk_t[:, None] - 1)
    score = jnp.where(forced[None, :, None, :], FORCE_SCORE,
                      jnp.where(causal[None, :, None, :], imp, -1.0))
    _, sel = lax.top_k(score, min(SLC_TOPN, n_slc))
    sel_valid = sel <= blk_t[None, :, None, None]

    qr = apply_rope(q, cos, sin)
    ks = apply_rope(k_slc, cos, sin)
    kw = apply_rope(k_win, cos, sin)

    ks_blk = ks.reshape(B, n_slc, SLC_BLOCK, NSA_GROUPS, HEAD_DIM).transpose(0, 3, 1, 2, 4)
    vs_blk = v_slc.reshape(B, n_slc, SLC_BLOCK, NSA_GROUPS, HEAD_DIM).transpose(0, 3, 1, 2, 4)
    bidx = jnp.arange(B)[:, None, None, None]
    gidx = jnp.arange(NSA_GROUPS)[None, None, :, None]

    def slc_block(i):
        qb = lax.dynamic_slice_in_dim(qr, i * SLC_Q_BLOCK, SLC_Q_BLOCK, axis=1)
        selb = lax.dynamic_slice_in_dim(sel, i * SLC_Q_BLOCK, SLC_Q_BLOCK, axis=1)
        validb = lax.dynamic_slice_in_dim(sel_valid, i * SLC_Q_BLOCK, SLC_Q_BLOCK, axis=1)
        qpos = i * SLC_Q_BLOCK + jnp.arange(SLC_Q_BLOCK)
        kg = ks_blk[bidx, gidx, selb]
        vg = vs_blk[bidx, gidx, selb]
        kpos = selb[..., None] * SLC_BLOCK + jnp.arange(SLC_BLOCK)
        mask = validb[..., None] & (kpos <= qpos[None, :, None, None, None])
        s = jnp.einsum('bqgjd,bqgnld->bqgjnl', qb, kg).astype(jnp.float32) * scale
        s = jnp.where(mask[:, :, :, None], s, NEG_INF)
        sh = s.shape
        p = jax.nn.softmax(s.reshape(sh[:4] + (-1,)), axis=-1).reshape(sh)
        return jnp.einsum('bqgjnl,bqgnld->bqgjd', p.astype(dt), vg)

    o_slc = _sweep(slc_block, T // SLC_Q_BLOCK)

    kw_pad = jnp.pad(kw, ((0, 0), (WIN, 0), (0, 0), (0, 0)))
    vw_pad = jnp.pad(v_win, ((0, 0), (WIN, 0), (0, 0), (0, 0)))

    def win_block(i):
        qb = lax.dynamic_slice_in_dim(qr, i * Q_BLOCK, Q_BLOCK, axis=1)
        qpos = i * Q_BLOCK + jnp.arange(Q_BLOCK)
        kb = lax.dynamic_slice_in_dim(kw_pad, i * Q_BLOCK, WIN + Q_BLOCK, axis=1)
        vb = lax.dynamic_slice_in_dim(vw_pad, i * Q_BLOCK, WIN + Q_BLOCK, axis=1)
        kpos = i * Q_BLOCK - WIN + jnp.arange(WIN + Q_BLOCK)
        mask = (kpos[None, :] >= 0) & (kpos[None, :] <= qpos[:, None]) & (kpos[None, :] > qpos[:, None] - WIN)
        s = jnp.einsum('bqgjd,bkgd->bqgjk', qb, kb).astype(jnp.float32) * scale
        s = jnp.where(mask[None, :, None, None, :], s, NEG_INF)
        p = jax.nn.softmax(s, axis=-1)
        return jnp.einsum('bqgjk,bkgd->bqgjd', p.astype(dt), vb)

    o_win = _sweep(win_block, T // Q_BLOCK)

    g = jax.nn.sigmoid(gate_logits.astype(jnp.float32)).astype(dt).reshape(B, T, NSA_GROUPS, NSA_HPG, 3)
    o = g[..., 0:1] * o_cmp + g[..., 1:2] * o_slc + g[..., 2:3] * o_win
    return o.reshape(B, T, NSA_Q_W)


def token_mix(h, w_in, da_lambda, da_head_norm, cmp_pos, cmp_w1, cmp_w2, w_proj_da, w_proj_nsa, w_out, lam_init):
    B, T = h.shape[0], h.shape[1]
    proj = h @ w_in
    offs = []
    acc = 0
    for w in IN_WIDTHS[:-1]:
        acc += w
        offs.append(acc)
    (q_da, k_da, v_da, q_ns, kc_raw, vc_raw, ks_raw, vs_raw, kw_raw, vw_raw,
     g_ns, g_a, g_b) = jnp.split(proj, offs, axis=-1)
    cos, sin = rope_tables(T, HEAD_DIM)
    y_da = diff_attention(q_da.reshape(B, T, DA_HEADS, 2, HEAD_DIM),
                          k_da.reshape(B, T, DA_HEADS, 2, HEAD_DIM),
                          v_da.reshape(B, T, DA_HEADS, DA_V_DIM),
                          cos, sin, da_lambda, da_head_norm, lam_init)
    kv = lambda a: a.reshape(B, T, NSA_GROUPS, HEAD_DIM)
    y_ns = nsa(q_ns.reshape(B, T, NSA_GROUPS, NSA_HPG, HEAD_DIM),
               kv(kc_raw), kv(vc_raw), kv(ks_raw), kv(vs_raw), kv(kw_raw), kv(vw_raw),
               g_ns, cos, sin, cmp_pos, cmp_w1, cmp_w2)
    merged = jax.nn.sigmoid(g_a) * (y_da @ w_proj_da) + jax.nn.sigmoid(g_b) * (y_ns @ w_proj_nsa)
    return merged @ w_out


def setup_inputs(seed: int = 0) -> dict:
    key = jax.random.key(seed)
    ks = jax.random.split(key, 24)
    f32 = jnp.float32

    def w(k, shape, fan_in):
        return jax.random.normal(k, shape, f32) * fan_in ** -0.5

    def gain(k, shape):
        return 1.0 + 0.05 * jax.random.normal(k, shape, f32)

    return {
        'x': jax.random.normal(ks[0], (BATCH, SEQ, D_MODEL), f32),
        'ffn1_norm': gain(ks[1], (DEPTH, D_MODEL)),
        'ffn1_w1': w(ks[2], (DEPTH, D_MODEL, D_FF), D_MODEL),
        'ffn1_w3': w(ks[3], (DEPTH, D_MODEL, D_FF), D_MODEL),
        'ffn1_w2': w(ks[4], (DEPTH, D_FF, D_MODEL), D_FF),
        'mix_norm': gain(ks[5], (DEPTH, D_MODEL)),
        'w_in': w(ks[6], (DEPTH, D_MODEL, D_IN), D_MODEL),
        'da_lambda': 0.1 * jax.random.normal(ks[7], (DEPTH, 4, HEAD_DIM), f32),
        'da_head_norm': gain(ks[8], (DEPTH, DA_HEADS, DA_V_DIM)),
        'cmp_pos': 0.1 * jax.random.normal(ks[9], (DEPTH, 2, CMP_BLOCK, HEAD_DIM), f32),
        'cmp_w1': w(ks[10], (DEPTH, 2, CMP_BLOCK * HEAD_DIM, CMP_HIDDEN), CMP_BLOCK * HEAD_DIM),
        'cmp_w2': w(ks[11], (DEPTH, 2, CMP_HIDDEN, HEAD_DIM), CMP_HIDDEN),
        'w_proj_da': w(ks[12], (DEPTH, DA_V_W, D_MODEL), DA_V_W),
        'w_proj_nsa': w(ks[13], (DEPTH, NSA_Q_W, D_MODEL), NSA_Q_W),
        'w_out': w(ks[14], (DEPTH, D_MODEL, D_MODEL), D_MODEL),
        'ffn2_norm': gain(ks[15], (DEPTH, D_MODEL)),
        'ffn2_w1': w(ks[16], (DEPTH, D_MODEL, D_FF), D_MODEL),
        'ffn2_w3': w(ks[17], (DEPTH, D_MODEL, D_FF), D_MODEL),
        'ffn2_w2': w(ks[18], (DEPTH, D_FF, D_MODEL), D_FF),
        'final_norm': gain(ks[19], (D_MODEL,)),
    }


def reference(x, ffn1_norm, ffn1_w1, ffn1_w3, ffn1_w2, mix_norm, w_in, da_lambda, da_head_norm,
              cmp_pos, cmp_w1, cmp_w2, w_proj_da, w_proj_nsa, w_out,
              ffn2_norm, ffn2_w1, ffn2_w3, ffn2_w2, final_norm):
    for l in range(DEPTH):
        lam_init = 0.8 - 0.6 * math.exp(-0.3 * l)
        h = rmsnorm(x, ffn1_norm[l])
        x = x + 0.5 * swiglu(h, ffn1_w1[l], ffn1_w3[l], ffn1_w2[l])
        h = rmsnorm(x, mix_norm[l])
        x = x + token_mix(h, w_in[l], da_lambda[l], da_head_norm[l], cmp_pos[l], cmp_w1[l], cmp_w2[l],
                          w_proj_da[l], w_proj_nsa[l], w_out[l], lam_init)
        h = rmsnorm(x, ffn2_norm[l])
        x = x + 0.5 * swiglu(h, ffn2_w1[l], ffn2_w3[l], ffn2_w2[l])
    return rmsnorm(x, final_norm)
```

```python
import functools
import math

import jax
import jax.numpy as jnp
from jax import lax
from jax.experimental import pallas as pl
from jax.experimental.pallas import tpu as pltpu

D_MODEL = 1024
HEAD_DIM = 64
DA_HEADS = 8
NSA_HEADS = 16
NSA_GROUPS = 2
NSA_HPG = NSA_HEADS // NSA_GROUPS
CMP_BLOCK = 32
CMP_STRIDE = 16
CMP_HIDDEN = 256
SLC_BLOCK = 64
SLC_TOPN = 16
WIN = 512
D_FF = 2816
ROPE_THETA = 10000.0
EPS = 1e-6
NEG_INF = -1e30
FORCE_SCORE = 1e9

LANES = 128
DA_W = DA_HEADS * 2 * HEAD_DIM
NSA_W = NSA_HEADS * HEAD_DIM
GROUP_W = NSA_HPG * HEAD_DIM
KV_W = NSA_GROUPS * HEAD_DIM
GATE_W = 3 * NSA_HEADS
GATE_PER_GROUP = 3 * NSA_HPG

TM_FFN = 512
TM_PROJ = 256
FF_CHUNK = 256
QB_DA = 256
QB_NSA = 128
CK_NSA = 256
VMEM_LIMIT = 56 * 1024 * 1024

BF16 = jnp.bfloat16
F32 = jnp.float32


def _rms(x, g):
    return x * lax.rsqrt(jnp.mean(x * x, axis=-1, keepdims=True) + EPS) * g


def _dot(a, b):
    return jnp.dot(a, b, preferred_element_type=F32)


def _dot_nt(a, b):
    return lax.dot_general(a, b, (((1,), (1,)), ((), ())), preferred_element_type=F32)


def _resident(shape):
    nd = len(shape)
    return pl.BlockSpec(shape, lambda *_: (0,) * nd, pipeline_mode=pl.Buffered(1))


def _params(*sem):
    return pltpu.CompilerParams(dimension_semantics=sem, vmem_limit_bytes=VMEM_LIMIT)


def _ffn_kernel(x_ref, g_ref, w1_ref, w3_ref, w2_ref, *rest, final):
    o_ref = rest[-1]
    x = x_ref[...]
    h = _rms(x, g_ref[...]).astype(BF16)
    acc = jnp.zeros_like(x)
    for c in range(D_FF // FF_CHUNK):
        sl = slice(c * FF_CHUNK, (c + 1) * FF_CHUNK)
        a = _dot(h, w1_ref[:, sl])
        b = _dot(h, w3_ref[:, sl])
        u = (a * jax.nn.sigmoid(a) * b).astype(BF16)
        acc = acc + _dot(u, w2_ref[sl, :])
    y = x + 0.5 * acc
    if final:
        y = _rms(y, rest[0][...])
    o_ref[...] = y


def _ffn(x2, g, w1, w3, w2, final_g=None):
    n = x2.shape[0]
    final = final_g is not None
    row = pl.BlockSpec((TM_FFN, D_MODEL), lambda i: (i, 0))
    ins = [x2, g.reshape(1, D_MODEL), w1.astype(BF16), w3.astype(BF16), w2.astype(BF16)]
    specs = [row, _resident((1, D_MODEL)), _resident((D_MODEL, D_FF)), _resident((D_MODEL, D_FF)),
             _resident((D_FF, D_MODEL))]
    if final:
        ins.append(final_g.reshape(1, D_MODEL))
        specs.append(_resident((1, D_MODEL)))
    return pl.pallas_call(
        functools.partial(_ffn_kernel, final=final),
        grid=(n // TM_FFN,),
        in_specs=specs,
        out_specs=row,
        out_shape=jax.ShapeDtypeStruct((n, D_MODEL), F32),
        compiler_params=_params("parallel"),
        name="ffn_final" if final else "ffn",
    )(*ins)


_C_QDA, _C_KDA, _C_VDA, _C_QNS = 0, 1024, 2048, 3072
_C_KCV = 4096
_C_KS2 = 4352
_C_VS2 = 4608
_C_KW2 = 4864
_C_VW2 = 5120
_C_GNS = 5376
_C_GA = 5632
_C_GB = 6656
_MAIN_W = 7680


def _proj_kernel(x_ref, g_ref, w_ref, cos_ref, sa_ref, sb_ref,
                 qda_ref, kda_ref, vda_ref, qnr_ref, qnc_ref, kcv_ref,
                 ks2_ref, vs2_ref, kw2_ref, vw2_ref, gns_ref, ga_ref, gb_ref):
    h = _rms(x_ref[0], g_ref[...]).astype(BF16)
    cos, sa, sb = cos_ref[...], sa_ref[...], sb_ref[...]

    def proj(c0, width):
        return _dot(h, w_ref[:, c0:c0 + width])

    def rope(y):
        cols = []
        for c in range(y.shape[1] // LANES):
            v = y[:, c * LANES:(c + 1) * LANES]
            cols.append(v * cos + pltpu.roll(v, LANES - 32, 1) * sa + pltpu.roll(v, 32, 1) * sb)
        return cols[0] if len(cols) == 1 else jnp.concatenate(cols, axis=1)

    scale = HEAD_DIM ** -0.5
    half = DA_W // 2
    for c0 in (0, half):
        qda_ref[0, :, c0:c0 + half] = (rope(proj(_C_QDA + c0, half)) * scale).astype(BF16)
        kda_ref[0, :, c0:c0 + half] = rope(proj(_C_KDA + c0, half)).astype(BF16)
        vda_ref[0, :, c0:c0 + half] = proj(_C_VDA + c0, half).astype(BF16)
        qn = proj(_C_QNS + c0, half)
        qnc_ref[0, :, c0:c0 + half] = (qn * scale).astype(BF16)
        qnr_ref[0, :, c0:c0 + half] = (rope(qn) * scale).astype(BF16)
        ga_ref[0, :, c0:c0 + half] = jax.nn.sigmoid(proj(_C_GA + c0, half))
        gb_ref[0, :, c0:c0 + half] = jax.nn.sigmoid(proj(_C_GB + c0, half))
    kcv_ref[0] = proj(_C_KCV, 256)
    ks2_ref[0] = rope(proj(_C_KS2, 256)).astype(BF16)
    vs2_ref[0] = proj(_C_VS2, 256).astype(BF16)
    kw2_ref[0] = rope(proj(_C_KW2, 256)).astype(BF16)
    vw2_ref[0] = proj(_C_VW2, 256).astype(BF16)
    gns_ref[0] = jax.nn.sigmoid(proj(_C_GNS, 256))


def _proj_weights(w_in):
    offs = [0]
    for w in (DA_W, DA_W, DA_W, NSA_W) + (KV_W,) * 6 + (GATE_W, D_MODEL, D_MODEL):
        offs.append(offs[-1] + w)
    (o_q, o_k, o_v, o_qn, o_kc, o_vc, o_ks, o_vs, o_kw, o_vw, o_g, o_ga, o_gb, _) = offs

    def dup(o):
        parts = []
        for g in range(NSA_GROUPS):
            blk = w_in[:, o + g * HEAD_DIM:o + (g + 1) * HEAD_DIM]
            parts += [blk, blk]
        return jnp.concatenate(parts, axis=1)

    gate_parts = []
    for g in range(NSA_GROUPS):
        gate_parts.append(jnp.pad(w_in[:, o_g + g * GATE_PER_GROUP:o_g + (g + 1) * GATE_PER_GROUP],
                                  ((0, 0), (0, LANES - GATE_PER_GROUP))))
    slab = jnp.concatenate(
        [w_in[:, o_q:o_kc], w_in[:, o_kc:o_ks], dup(o_ks), dup(o_vs), dup(o_kw), dup(o_vw)]
        + gate_parts + [w_in[:, o_ga:o_gb], w_in[:, o_gb:]], axis=1)
    return slab.astype(BF16)


def _rope_tables(t):
    pos = jnp.arange(t, dtype=F32)
    inv = 1.0 / (ROPE_THETA ** (jnp.arange(0, HEAD_DIM, 2, dtype=F32) / HEAD_DIM))
    ang = pos[:, None] * inv[None, :]
    cos, sin = jnp.cos(ang), jnp.sin(ang)
    zero = jnp.zeros_like(sin)
    cos128 = jnp.tile(cos, (1, 4))
    sa = jnp.tile(jnp.concatenate([-sin, zero], axis=1), (1, 2))
    sb = jnp.tile(jnp.concatenate([zero, sin], axis=1), (1, 2))
    return cos128, sa, sb


def _inproj(x3, g, w_in):
    b, t, _ = x3.shape
    cos, sa, sb = _rope_tables(t)
    tok = lambda w: pl.BlockSpec((1, TM_PROJ, w), lambda i, j: (i, j, 0))
    tab = pl.BlockSpec((TM_PROJ, LANES), lambda i, j: (j, 0))
    sds = lambda w, dt: jax.ShapeDtypeStruct((b, t, w), dt)
    outs = [(DA_W, BF16)] * 3 + [(NSA_W, BF16)] * 2 + [(256, F32)] + [(256, BF16)] * 4 \
        + [(256, F32), (D_MODEL, F32), (D_MODEL, F32)]
    return pl.pallas_call(
        _proj_kernel,
        grid=(b, t // TM_PROJ),
        in_specs=[tok(D_MODEL), _resident((1, D_MODEL)), _resident((D_MODEL, _MAIN_W)), tab, tab, tab],
        out_specs=[tok(w) for w, _ in outs],
        out_shape=[sds(w, dt) for w, dt in outs],
        compiler_params=_params("parallel", "parallel"),
        name="inproj",
    )(x3, g.reshape(1, D_MODEL), _proj_weights(w_in), cos, sa, sb)


def _online_update(state, s, v):
    m, l, acc = state
    m_new = jnp.maximum(m, jnp.max(s, axis=1, keepdims=True))
    alpha = jnp.exp(m - m_new)
    e = jnp.exp(s - m_new)
    l = alpha * l + jnp.sum(e, axis=1, keepdims=True)
    acc = alpha * acc + _dot(e.astype(BF16), v)
    return m_new, l, acc


def _da_kernel(lam_ref, gain_ref, q_ref, k_ref, v_ref, o_ref, *, lam_init):
    t = q_ref.shape[1]
    qb = QB_DA
    lp = lam_ref[...]
    lam = (jnp.exp(jnp.sum(lp[0:1] * lp[1:2], axis=1, keepdims=True))
           - jnp.exp(jnp.sum(lp[2:3] * lp[3:4], axis=1, keepdims=True)) + lam_init)
    gain = gain_ref[0] * (1.0 - lam_init)
    lane = lax.broadcasted_iota(jnp.int32, (qb, LANES), 1)
    tri = (lax.broadcasted_iota(jnp.int32, (qb, qb), 1) <= lax.broadcasted_iota(jnp.int32, (qb, qb), 0))
    diag_bias = jnp.where(tri, 0.0, NEG_INF)[None]

    def qblock(qi, carry):
        r0 = pl.multiple_of(qi * qb, qb)
        q = q_ref[0, pl.ds(r0, qb), :]
        zero = jnp.zeros_like(q)
        qs = jnp.concatenate([jnp.where(lane < HEAD_DIM, q, zero), jnp.where(lane >= HEAD_DIM, q, zero)], axis=0)

        def chunk(c, state):
            k0 = pl.multiple_of(c * qb, qb)
            s = _dot_nt(qs, k_ref[0, pl.ds(k0, qb), :])
            return _online_update(state, s, v_ref[0, pl.ds(k0, qb), :])

        init = (jnp.full((2 * qb, 1), NEG_INF, F32), jnp.zeros((2 * qb, 1), F32), jnp.zeros((2 * qb, LANES), F32))
        state = lax.fori_loop(0, qi, chunk, init)
        s = _dot_nt(qs, k_ref[0, pl.ds(r0, qb), :])
        s = (s.reshape(2, qb, qb) + diag_bias).reshape(2 * qb, qb)
        _, l, acc = _online_update(state, s, v_ref[0, pl.ds(r0, qb), :])
        o = acc / l
        o = o[:qb] - lam * o[qb:]
        o_ref[0, pl.ds(r0, qb), :] = (_rms(o, gain)).astype(o_ref.dtype)
        return carry

    lax.fori_loop(0, t // qb, qblock, 0)


def _diff_attention(q, k, v, da_lambda, head_gain, lam_init):
    b, t, _ = q.shape
    head = pl.BlockSpec((1, t, LANES), lambda i, h: (i, 0, h))
    return pl.pallas_call(
        functools.partial(_da_kernel, lam_init=lam_init),
        grid=(b, DA_HEADS),
        in_specs=[pl.BlockSpec((4, HEAD_DIM), lambda i, h: (0, 0)),
                  pl.BlockSpec((1, 1, LANES), lambda i, h: (h, 0, 0)), head, head, head],
        out_specs=head,
        out_shape=jax.ShapeDtypeStruct((b, t, DA_W), BF16),
        compiler_params=_params("parallel", "parallel"),
        name="diffattn",
    )(da_lambda, head_gain.reshape(DA_HEADS, 1, LANES), q, k, v)


def _cmp_kernel(x_ref, pos_ref, w1_ref, w2_ref, o_ref):
    half = CMP_STRIDE * HEAD_DIM
    for kv in range(2):
        for g in range(NSA_GROUPS):
            x = x_ref[0, kv * NSA_GROUPS + g]
            pos = pos_ref[kv]
            xa = (x + pos[:, :half]).astype(BF16)
            xb = (x + pos[:, half:]).astype(BF16)
            a = _dot(xa, w1_ref[kv, :half, :])
            bm = _dot(xb, w1_ref[kv, half:, :])
            hid = a + pltpu.roll(bm, bm.shape[0] - 1, 0)
            act = jax.nn.gelu(hid).astype(BF16)
            o_ref[0, kv * NSA_GROUPS + g] = _dot(act, w2_ref[kv]).astype(o_ref.dtype)


def _compress(kcv, cmp_pos, cmp_w1, cmp_w2):
    b, t, _ = kcv.shape
    n = t // CMP_STRIDE
    half = CMP_STRIDE * HEAD_DIM
    xc = kcv.reshape(b, n, CMP_STRIDE, 2 * NSA_GROUPS, HEAD_DIM).transpose(0, 3, 1, 2, 4).reshape(
        b, 2 * NSA_GROUPS, n, half)
    w2d = jnp.concatenate([cmp_w2, cmp_w2], axis=-1).astype(BF16)
    return pl.pallas_call(
        _cmp_kernel,
        grid=(b,),
        in_specs=[pl.BlockSpec((1, 2 * NSA_GROUPS, n, half), lambda i: (i, 0, 0, 0)),
                  pl.BlockSpec((2, 1, CMP_BLOCK * HEAD_DIM), lambda i: (0, 0, 0)),
                  pl.BlockSpec((2, CMP_BLOCK * HEAD_DIM, CMP_HIDDEN), lambda i: (0, 0, 0)),
                  pl.BlockSpec((2, CMP_HIDDEN, LANES), lambda i: (0, 0, 0))],
        out_specs=pl.BlockSpec((1, 2 * NSA_GROUPS, n, LANES), lambda i: (i, 0, 0, 0)),
        out_shape=jax.ShapeDtypeStruct((b, 2 * NSA_GROUPS, n, LANES), BF16),
        compiler_params=_params("parallel"),
        name="compress",
    )(xc, cmp_pos.reshape(2, 1, CMP_BLOCK * HEAD_DIM), cmp_w1.astype(BF16), w2d)


def _split3(x):
    hi = x.astype(BF16)
    r1 = x - hi.astype(F32)
    mid = r1.astype(BF16)
    lo = (r1 - mid.astype(F32)).astype(BF16)
    return hi, mid, lo


def _nsa_kernel(qr_ref, qc_ref, kc_ref, vc_ref, ks_ref, vs_ref, kw_ref, vw_ref, gate_ref, o_ref):
    t = qr_ref.shape[1]
    qb, ck, hp = QB_NSA, CK_NSA, NSA_HPG
    n_slc_pad = LANES
    n_slc = t // SLC_BLOCK
    win_keys = WIN + qb
    lane = lax.broadcasted_iota(jnp.int32, (qb, LANES), 1)
    lo_lane = lane < HEAD_DIM
    row_q = lax.broadcasted_iota(jnp.int32, (qb, LANES), 0)

    ov_s = lax.broadcasted_iota(jnp.int32, (n_slc_pad, LANES), 0)
    ov_n = lax.broadcasted_iota(jnp.int32, (n_slc_pad, LANES), 1)
    ov_t = jnp.where((ov_n * CMP_STRIDE < (ov_s + 1) * SLC_BLOCK) & ((ov_n * CMP_STRIDE + CMP_BLOCK) > ov_s * SLC_BLOCK)
                     & (ov_s < n_slc), 1.0, 0.0).astype(BF16)
    s_idx = lax.broadcasted_iota(jnp.int32, (n_slc, qb), 0)
    t_off = lax.broadcasted_iota(jnp.int32, (n_slc, qb), 1)
    e_s = lax.broadcasted_iota(jnp.int32, (n_slc_pad, ck), 0)
    e_key = lax.broadcasted_iota(jnp.int32, (n_slc_pad, ck), 1)
    col_ck = lax.broadcasted_iota(jnp.int32, (qb, ck), 1)
    row_ck = lax.broadcasted_iota(jnp.int32, (qb, ck), 0)
    col_w = lax.broadcasted_iota(jnp.int32, (qb, win_keys), 1)
    row_w = lax.broadcasted_iota(jnp.int32, (qb, win_keys), 0)

    def stack_heads(qblk):
        zero = jnp.zeros((qb, LANES), qblk.dtype)
        parts = []
        for j in range(hp):
            pair = qblk[:, (j // 2) * LANES:(j // 2 + 1) * LANES]
            parts.append(jnp.where(lo_lane if j % 2 == 0 else ~lo_lane, pair, zero))
        return jnp.concatenate(parts, axis=0)

    def qblock(qi, carry):
        r0 = pl.multiple_of(qi * qb, qb)
        q_c = stack_heads(qc_ref[0, pl.ds(r0, qb), :])
        q_r = stack_heads(qr_ref[0, pl.ds(r0, qb), :])

        s = _dot_nt(q_c, kc_ref[0, 0])
        valid = (lane * CMP_STRIDE + (CMP_BLOCK - 1)) <= (r0 + row_q)
        s = (s.reshape(hp, qb, LANES) + jnp.where(valid, 0.0, NEG_INF)[None])
        e = jnp.exp(s - jnp.max(s, axis=2, keepdims=True))
        p = jnp.where(valid[None], e / jnp.sum(e, axis=2, keepdims=True), 0.0)
        o_cmp = _dot(p.reshape(hp * qb, LANES).astype(BF16), vc_ref[0, 0])

        psum = jnp.sum(p, axis=0)
        imp_t = sum(_dot_nt(ov_t, term) for term in _split3(psum))
        blk_t = (r0 + t_off) // SLC_BLOCK
        causal = s_idx <= blk_t
        forced = (s_idx == 0) | (s_idx == blk_t) | (s_idx == blk_t - 1)
        score = jnp.where(forced, FORCE_SCORE, jnp.where(causal, imp_t[:n_slc], -1.0))
        rank = jnp.zeros((n_slc, qb), F32)
        for sp in range(n_slc):
            other = score[sp:sp + 1, :]
            tie = jnp.where(s_idx > sp, 1.0, 0.0)
            rank = rank + jnp.where(other > score, 1.0, jnp.where(other == score, tie, 0.0))
        sel_t = jnp.where(causal, jnp.where(rank < SLC_TOPN, 1.0, 0.0), 0.0)
        sel_t = jnp.concatenate([sel_t, jnp.zeros((n_slc_pad - n_slc, qb), F32)], axis=0)
        sel = sel_t.T.astype(BF16)

        def chunk(c, state):
            k0 = pl.multiple_of(c * ck, ck)
            sc = _dot_nt(q_r, ks_ref[0, pl.ds(k0, ck), :])
            expand = jnp.where(e_s == (k0 + e_key) // SLC_BLOCK, 1.0, 0.0).astype(BF16)
            chosen = _dot(sel, expand)
            bias = jnp.where((k0 + col_ck) <= (r0 + row_ck), jnp.where(chosen > 0.5, 0.0, NEG_INF), NEG_INF)
            sc = (sc.reshape(hp, qb, ck) + bias[None]).reshape(hp * qb, ck)
            return _online_update(state, sc, vs_ref[0, pl.ds(k0, ck), :])

        init = (jnp.full((hp * qb, 1), NEG_INF, F32), jnp.zeros((hp * qb, 1), F32),
                jnp.zeros((hp * qb, LANES), F32))
        _, l_s, acc_s = lax.fori_loop(0, (r0 + qb + ck - 1) // ck, chunk, init)
        o_slc = acc_s / l_s

        w0 = pl.multiple_of(jnp.maximum(r0 - WIN, 0), qb)
        sw = _dot_nt(q_r, kw_ref[0, pl.ds(w0, win_keys), :])
        kpos, qpos = w0 + col_w, r0 + row_w
        bias_w = jnp.where(kpos <= qpos, jnp.where(kpos > qpos - WIN, 0.0, NEG_INF), NEG_INF)
        sw = sw.reshape(hp, qb, win_keys) + bias_w[None]
        ew = jnp.exp(sw - jnp.max(sw, axis=2, keepdims=True))
        l_w = jnp.sum(ew, axis=2, keepdims=True).reshape(hp * qb, 1)
        o_win = _dot(ew.reshape(hp * qb, win_keys).astype(BF16), vw_ref[0, pl.ds(w0, win_keys), :]) / l_w

        gate = gate_ref[0, pl.ds(r0, qb), :]
        outs = []
        for j in range(hp):
            rows = slice(j * qb, (j + 1) * qb)
            outs.append(gate[:, 3 * j:3 * j + 1] * o_cmp[rows] + gate[:, 3 * j + 1:3 * j + 2] * o_slc[rows]
                        + gate[:, 3 * j + 2:3 * j + 3] * o_win[rows])
        pairs = [jnp.where(lo_lane, outs[2 * m], outs[2 * m + 1]) for m in range(hp // 2)]
        o_ref[0, pl.ds(r0, qb), :] = jnp.concatenate(pairs, axis=1).astype(o_ref.dtype)
        return carry

    lax.fori_loop(0, t // qb, qblock, 0)


def _nsa(qr, qc, kvc, ks2, vs2, kw2, vw2, gns):
    b, t, _ = qr.shape
    n = t // CMP_STRIDE
    grp = lambda w: pl.BlockSpec((1, t, w), lambda i, g: (i, 0, g))
    cmp_k = pl.BlockSpec((1, 1, n, LANES), lambda i, g: (i, g, 0, 0))
    cmp_v = pl.BlockSpec((1, 1, n, LANES), lambda i, g: (i, NSA_GROUPS + g, 0, 0))
    return pl.pallas_call(
        _nsa_kernel,
        grid=(b, NSA_GROUPS),
        in_specs=[grp(GROUP_W), grp(GROUP_W), cmp_k, cmp_v, grp(LANES), grp(LANES), grp(LANES), grp(LANES),
                  grp(LANES)],
        out_specs=grp(GROUP_W),
        out_shape=jax.ShapeDtypeStruct((b, t, NSA_W), BF16),
        compiler_params=_params("parallel", "parallel"),
        name="nsa",
    )(qr, qc, kvc, kvc, ks2, vs2, kw2, vw2, gns)


def _merge_kernel(x_ref, yda_ref, yns_ref, ga_ref, gb_ref, wa_ref, wb_ref, wo_ref, o_ref):
    merged = ga_ref[...] * _dot(yda_ref[...], wa_ref[...]) + gb_ref[...] * _dot(yns_ref[...], wb_ref[...])
    o_ref[...] = x_ref[...] + _dot(merged.astype(BF16), wo_ref[...])


def _merge(x2, yda, yns, ga, gb, w_da, w_nsa, w_out):
    n = x2.shape[0]
    row = pl.BlockSpec((TM_FFN, D_MODEL), lambda i: (i, 0))
    wspec = _resident((D_MODEL, D_MODEL))
    return pl.pallas_call(
        _merge_kernel,
        grid=(n // TM_FFN,),
        in_specs=[row] * 5 + [wspec] * 3,
        out_specs=row,
        out_shape=jax.ShapeDtypeStruct((n, D_MODEL), F32),
        compiler_params=_params("parallel"),
        name="merge",
    )(x2, yda, yns, ga, gb, w_da.astype(BF16), w_nsa.astype(BF16), w_out.astype(BF16))


def kernel(x, ffn1_norm, ffn1_w1, ffn1_w3, ffn1_w2, mix_norm, w_in, da_lambda, da_head_norm, cmp_pos, cmp_w1,
           cmp_w2, w_proj_da, w_proj_nsa, w_out, ffn2_norm, ffn2_w1, ffn2_w3, ffn2_w2, final_norm):
    b, t, d = x.shape
    depth = ffn1_norm.shape[0]
    x2 = x.reshape(b * t, d)
    for l in range(depth):
        lam_init = 0.8 - 0.6 * math.exp(-0.3 * l)
        x2 = _ffn(x2, ffn1_norm[l], ffn1_w1[l], ffn1_w3[l], ffn1_w2[l])
        (qda, kda, vda, qnr, qnc, kcv, ks2, vs2, kw2, vw2, gns, ga, gb) = _inproj(
            x2.reshape(b, t, d), mix_norm[l], w_in[l])
        yda = _diff_attention(qda, kda, vda, da_lambda[l], da_head_norm[l], lam_init)
        kvc = _compress(kcv, cmp_pos[l], cmp_w1[l], cmp_w2[l])
        yns = _nsa(qnr, qnc, kvc, ks2, vs2, kw2, vw2, gns)
        flat = lambda a: a.reshape(b * t, a.shape[-1])
        x2 = _merge(x2, flat(yda), flat(yns), flat(ga), flat(gb), w_proj_da[l], w_proj_nsa[l], w_out[l])
        last = l == depth - 1
        x2 = _ffn(x2, ffn2_norm[l], ffn2_w1[l], ffn2_w3[l], ffn2_w2[l], final_norm if last else None)
    if depth == 0:
        raise ValueError("depth must be positive")
    return x2.reshape(b, t, d)
```

```python
import functools
import math

import jax
import jax.numpy as jnp
from jax import lax
from jax.experimental import pallas as pl
from jax.experimental.pallas import tpu as pltpu

D_MODEL = 1024
HEAD_DIM = 64
DA_HEADS = 8
NSA_HEADS = 16
NSA_GROUPS = 2
NSA_HPG = NSA_HEADS // NSA_GROUPS
CMP_BLOCK = 32
CMP_STRIDE = 16
CMP_HIDDEN = 256
SLC_BLOCK = 64
SLC_TOPN = 16
WIN = 512
D_FF = 2816
ROPE_THETA = 10000.0
EPS = 1e-6
NEG_INF = -1e30
FORCE_SCORE = 1e9
LOG2E = math.log2(math.e)

LANES = 128
DA_W = DA_HEADS * 2 * HEAD_DIM
NSA_W = NSA_HEADS * HEAD_DIM
GROUP_W = NSA_HPG * HEAD_DIM
PAIRS = NSA_HPG // 2
KV_W = NSA_GROUPS * HEAD_DIM
GATE_W = 3 * NSA_HEADS
GATE_PER_GROUP = 3 * NSA_HPG
SLC_SHIFT = SLC_BLOCK.bit_length() - 1
assert 1 << SLC_SHIFT == SLC_BLOCK

TM_FFN = 512
TM_PROJ = 256
FF_CHUNK = 256
QB_DA = 256
QB_NSA = 256
VMEM_LIMIT = 56 * 1024 * 1024

BF16 = jnp.bfloat16
F32 = jnp.float32


def _rms(x, g):
    return x * lax.rsqrt(jnp.mean(x * x, axis=-1, keepdims=True) + EPS) * g


def _dot(a, b):
    return jnp.dot(a, b, preferred_element_type=F32)


def _dot_nt(a, b):
    return lax.dot_general(a, b, (((1,), (1,)), ((), ())), preferred_element_type=F32)


def _resident(shape):
    nd = len(shape)
    return pl.BlockSpec(shape, lambda *_: (0,) * nd, pipeline_mode=pl.Buffered(1))


def _params(*sem):
    return pltpu.CompilerParams(dimension_semantics=sem, vmem_limit_bytes=VMEM_LIMIT)


def _ffn_kernel(x_ref, g_ref, w1_ref, w3_ref, w2_ref, *rest, final):
    o_ref = rest[-1]
    x = x_ref[...]
    h = _rms(x, g_ref[...]).astype(BF16)
    acc = jnp.zeros_like(x)
    for c in range(D_FF // FF_CHUNK):
        sl = slice(c * FF_CHUNK, (c + 1) * FF_CHUNK)
        a = _dot(h, w1_ref[:, sl])
        b = _dot(h, w3_ref[:, sl])
        u = (a * jax.nn.sigmoid(a) * b).astype(BF16)
        acc = acc + _dot(u, w2_ref[sl, :])
    y = x + 0.5 * acc
    if final:
        y = _rms(y, rest[0][...])
    o_ref[...] = y


def _ffn(x2, g, w1, w3, w2, final_g=None):
    n = x2.shape[0]
    final = final_g is not None
    row = pl.BlockSpec((TM_FFN, D_MODEL), lambda i: (i, 0))
    ins = [x2, g.reshape(1, D_MODEL), w1.astype(BF16), w3.astype(BF16), w2.astype(BF16)]
    specs = [row, _resident((1, D_MODEL)), _resident((D_MODEL, D_FF)), _resident((D_MODEL, D_FF)),
             _resident((D_FF, D_MODEL))]
    if final:
        ins.append(final_g.reshape(1, D_MODEL))
        specs.append(_resident((1, D_MODEL)))
    return pl.pallas_call(
        functools.partial(_ffn_kernel, final=final),
        grid=(n // TM_FFN,),
        in_specs=specs,
        out_specs=row,
        out_shape=jax.ShapeDtypeStruct((n, D_MODEL), F32),
        compiler_params=_params("parallel"),
        name="ffn_final" if final else "ffn",
    )(*ins)


_C_QDA, _C_KDA, _C_VDA, _C_QNS = 0, 1024, 2048, 3072
_C_KCV = 4096
_C_KS2 = 4352
_C_VS2 = 4608
_C_KW2 = 4864
_C_VW2 = 5120
_C_GNS = 5376
_C_GA = 5632
_C_GB = 6656
_MAIN_W = 7680


def _proj_kernel(x_ref, g_ref, w_ref, cos_ref, sa_ref, sb_ref,
                 qda_ref, kda_ref, vda_ref, qnr_ref, qnc_ref, kcv_ref,
                 ks2_ref, vs2_ref, kw2_ref, vw2_ref, gns_ref, ga_ref, gb_ref):
    h = _rms(x_ref[0], g_ref[...]).astype(BF16)
    cos, sa, sb = cos_ref[...], sa_ref[...], sb_ref[...]

    def proj(c0, width):
        return _dot(h, w_ref[:, c0:c0 + width])

    def rope(y):
        cols = []
        for c in range(y.shape[1] // LANES):
            v = y[:, c * LANES:(c + 1) * LANES]
            cols.append(v * cos + pltpu.roll(v, LANES - 32, 1) * sa + pltpu.roll(v, 32, 1) * sb)
        return cols[0] if len(cols) == 1 else jnp.concatenate(cols, axis=1)

    qscale = HEAD_DIM ** -0.5 * LOG2E
    half = DA_W // 2
    for c0 in (0, half):
        qda_ref[0, :, c0:c0 + half] = (rope(proj(_C_QDA + c0, half)) * qscale).astype(BF16)
        kda_ref[0, :, c0:c0 + half] = rope(proj(_C_KDA + c0, half)).astype(BF16)
        vda_ref[0, :, c0:c0 + half] = proj(_C_VDA + c0, half).astype(BF16)
        qn = proj(_C_QNS + c0, half)
        qnc_ref[0, :, c0:c0 + half] = (qn * qscale).astype(BF16)
        qnr_ref[0, :, c0:c0 + half] = (rope(qn) * qscale).astype(BF16)
        ga_ref[0, :, c0:c0 + half] = jax.nn.sigmoid(proj(_C_GA + c0, half))
        gb_ref[0, :, c0:c0 + half] = jax.nn.sigmoid(proj(_C_GB + c0, half))
    kcv_ref[0] = proj(_C_KCV, 256)
    ks2_ref[0] = rope(proj(_C_KS2, 256)).astype(BF16)
    vs2_ref[0] = proj(_C_VS2, 256).astype(BF16)
    kw2_ref[0] = rope(proj(_C_KW2, 256)).astype(BF16)
    vw2_ref[0] = proj(_C_VW2, 256).astype(BF16)
    gns_ref[0] = jax.nn.sigmoid(proj(_C_GNS, 256))


def _proj_weights(w_in):
    offs = [0]
    for w in (DA_W, DA_W, DA_W, NSA_W) + (KV_W,) * 6 + (GATE_W, D_MODEL, D_MODEL):
        offs.append(offs[-1] + w)
    (o_q, o_k, o_v, o_qn, o_kc, o_vc, o_ks, o_vs, o_kw, o_vw, o_g, o_ga, o_gb, _) = offs

    def dup(o):
        parts = []
        for g in range(NSA_GROUPS):
            blk = w_in[:, o + g * HEAD_DIM:o + (g + 1) * HEAD_DIM]
            parts += [blk, blk]
        return jnp.concatenate(parts, axis=1)

    gate_parts = []
    for g in range(NSA_GROUPS):
        gate_parts.append(jnp.pad(w_in[:, o_g + g * GATE_PER_GROUP:o_g + (g + 1) * GATE_PER_GROUP],
                                  ((0, 0), (0, LANES - GATE_PER_GROUP))))
    slab = jnp.concatenate(
        [w_in[:, o_q:o_kc], w_in[:, o_kc:o_ks], dup(o_ks), dup(o_vs), dup(o_kw), dup(o_vw)]
        + gate_parts + [w_in[:, o_ga:o_gb], w_in[:, o_gb:]], axis=1)
    return slab.astype(BF16)


def _rope_tables(t):
    pos = jnp.arange(t, dtype=F32)
    inv = 1.0 / (ROPE_THETA ** (jnp.arange(0, HEAD_DIM, 2, dtype=F32) / HEAD_DIM))
    ang = pos[:, None] * inv[None, :]
    cos, sin = jnp.cos(ang), jnp.sin(ang)
    zero = jnp.zeros_like(sin)
    cos128 = jnp.tile(cos, (1, 4))
    sa = jnp.tile(jnp.concatenate([-sin, zero], axis=1), (1, 2))
    sb = jnp.tile(jnp.concatenate([zero, sin], axis=1), (1, 2))
    return cos128, sa, sb


def _inproj(x3, g, w_in):
    b, t, _ = x3.shape
    cos, sa, sb = _rope_tables(t)
    tok = lambda w: pl.BlockSpec((1, TM_PROJ, w), lambda i, j: (i, j, 0))
    tab = pl.BlockSpec((TM_PROJ, LANES), lambda i, j: (j, 0))
    sds = lambda w, dt: jax.ShapeDtypeStruct((b, t, w), dt)
    outs = [(DA_W, BF16)] * 3 + [(NSA_W, BF16)] * 2 + [(256, F32)] + [(256, BF16)] * 4 \
        + [(256, F32), (D_MODEL, F32), (D_MODEL, F32)]
    return pl.pallas_call(
        _proj_kernel,
        grid=(b, t // TM_PROJ),
        in_specs=[tok(D_MODEL), _resident((1, D_MODEL)), _resident((D_MODEL, _MAIN_W)), tab, tab, tab],
        out_specs=[tok(w) for w, _ in outs],
        out_shape=[sds(w, dt) for w, dt in outs],
        compiler_params=_params("parallel", "parallel"),
        name="inproj",
    )(x3, g.reshape(1, D_MODEL), _proj_weights(w_in), cos, sa, sb)


def _fill_value_slab(va_ref, v):
    va_ref[:, :LANES] = v
    va_ref[:, LANES:] = jnp.ones(v.shape, v.dtype)


def _causal_bias(qb):
    tri = lax.broadcasted_iota(jnp.int32, (qb, qb), 1) <= lax.broadcasted_iota(jnp.int32, (qb, qb), 0)
    return jnp.where(tri, 0.0, NEG_INF)


def _da_kernel(lam_ref, gain_ref, q_ref, k_ref, v_ref, o_ref, va_ref, *, lam_init):
    t = q_ref.shape[1]
    qb = QB_DA
    _fill_value_slab(va_ref, v_ref[0])
    lp = lam_ref[...]
    lam = (jnp.exp(jnp.sum(lp[0:1] * lp[1:2], axis=1, keepdims=True))
           - jnp.exp(jnp.sum(lp[2:3] * lp[3:4], axis=1, keepdims=True)) + lam_init)
    gain = gain_ref[0] * (1.0 - lam_init)
    lane = lax.broadcasted_iota(jnp.int32, (qb, LANES), 1)
    diag_bias = _causal_bias(qb)

    for i in range(t // qb):
        r0 = i * qb
        q = q_ref[0, r0:r0 + qb, :]
        zero = jnp.zeros_like(q)
        qs = jnp.concatenate([jnp.where(lane < HEAD_DIM, q, zero), jnp.where(lane >= HEAD_DIM, q, zero)], axis=0)
        sd = _dot_nt(qs, k_ref[0, r0:r0 + qb, :]).reshape(2, qb, qb) + diag_bias[None]
        m = jnp.max(sd, axis=2, keepdims=True)
        if i:
            sm = _dot_nt(qs, k_ref[0, :r0, :]).reshape(2, qb, r0)
            m = jnp.maximum(m, jnp.max(sm, axis=2, keepdims=True))
            pm = jnp.exp2(sm - m).astype(BF16).reshape(2 * qb, r0)
        pd = jnp.exp2(sd - m).astype(BF16).reshape(2 * qb, qb)
        oa = _dot(pd, va_ref[r0:r0 + qb, :])
        if i:
            oa = oa + _dot(pm, va_ref[:r0, :])
        o = oa[:, :LANES] / oa[:, LANES:]
        o = o[:qb] - lam * o[qb:]
        o_ref[0, r0:r0 + qb, :] = (_rms(o, gain)).astype(o_ref.dtype)


def _diff_attention(q, k, v, da_lambda, head_gain, lam_init):
    b, t, _ = q.shape
    head = pl.BlockSpec((1, t, LANES), lambda i, h: (i, 0, h))
    return pl.pallas_call(
        functools.partial(_da_kernel, lam_init=lam_init),
        grid=(b, DA_HEADS),
        in_specs=[pl.BlockSpec((4, HEAD_DIM), lambda i, h: (0, 0)),
                  pl.BlockSpec((1, 1, LANES), lambda i, h: (h, 0, 0)), head, head, head],
        out_specs=head,
        out_shape=jax.ShapeDtypeStruct((b, t, DA_W), BF16),
        scratch_shapes=[pltpu.VMEM((t, 2 * LANES), BF16)],
        compiler_params=_params("parallel", "parallel"),
        name="diffattn",
    )(da_lambda, head_gain.reshape(DA_HEADS, 1, LANES), q, k, v)


def _cmp_kernel(x_ref, pos_ref, w1_ref, w2_ref, o_ref):
    half = CMP_STRIDE * HEAD_DIM
    for kv in range(2):
        for g in range(NSA_GROUPS):
            x = x_ref[0, kv * NSA_GROUPS + g]
            pos = pos_ref[kv]
            xa = (x + pos[:, :half]).astype(BF16)
            xb = (x + pos[:, half:]).astype(BF16)
            a = _dot(xa, w1_ref[kv, :half, :])
            bm = _dot(xb, w1_ref[kv, half:, :])
            hid = a + pltpu.roll(bm, bm.shape[0] - 1, 0)
            act = jax.nn.gelu(hid).astype(BF16)
            o_ref[0, kv * NSA_GROUPS + g] = _dot(act, w2_ref[kv]).astype(o_ref.dtype)


def _compress(kcv, cmp_pos, cmp_w1, cmp_w2):
    b, t, _ = kcv.shape
    n = t // CMP_STRIDE
    half = CMP_STRIDE * HEAD_DIM
    xc = kcv.reshape(b, n, CMP_STRIDE, 2 * NSA_GROUPS, HEAD_DIM).transpose(0, 3, 1, 2, 4).reshape(
        b, 2 * NSA_GROUPS, n, half)
    w2d = jnp.concatenate([cmp_w2, cmp_w2], axis=-1).astype(BF16)
    return pl.pallas_call(
        _cmp_kernel,
        grid=(b,),
        in_specs=[pl.BlockSpec((1, 2 * NSA_GROUPS, n, half), lambda i: (i, 0, 0, 0)),
                  pl.BlockSpec((2, 1, CMP_BLOCK * HEAD_DIM), lambda i: (0, 0, 0)),
                  pl.BlockSpec((2, CMP_BLOCK * HEAD_DIM, CMP_HIDDEN), lambda i: (0, 0, 0)),
                  pl.BlockSpec((2, CMP_HIDDEN, LANES), lambda i: (0, 0, 0))],
        out_specs=pl.BlockSpec((1, 2 * NSA_GROUPS, n, LANES), lambda i: (i, 0, 0, 0)),
        out_shape=jax.ShapeDtypeStruct((b, 2 * NSA_GROUPS, n, LANES), BF16),
        compiler_params=_params("parallel"),
        name="compress",
    )(xc, cmp_pos.reshape(2, 1, CMP_BLOCK * HEAD_DIM), cmp_w1.astype(BF16), w2d)


def _split3(x):
    hi = x.astype(BF16)
    r1 = x - hi.astype(F32)
    mid = r1.astype(BF16)
    lo = (r1 - mid.astype(F32)).astype(BF16)
    return hi, mid, lo


def _stack_heads(qblk, heads, lo_lane):
    zero = jnp.zeros((qblk.shape[0], LANES), qblk.dtype)
    parts = []
    for j in heads:
        pair = qblk[:, (j // 2) * LANES:(j // 2 + 1) * LANES]
        parts.append(jnp.where(lo_lane if j % 2 == 0 else ~lo_lane, pair, zero))
    return jnp.concatenate(parts, axis=0)


def _cmp_valid_bias(r0, qb):
    lane = lax.broadcasted_iota(jnp.int32, (qb, LANES), 1)
    row = lax.broadcasted_iota(jnp.int32, (qb, LANES), 0)
    valid = (lane * CMP_STRIDE + (CMP_BLOCK - 1)) <= (r0 + row)
    return valid, jnp.where(valid, 0.0, NEG_INF)


def _sel_kernel(qc_ref, kc_ref, o_ref):
    t = qc_ref.shape[1]
    qb, hp = QB_NSA, NSA_HPG
    n_slc = t // SLC_BLOCK
    lane = lax.broadcasted_iota(jnp.int32, (qb, LANES), 1)
    lo_lane = lane < HEAD_DIM
    bias_lane = (lane & (HEAD_DIM - 1)) < n_slc
    ov_s = lax.broadcasted_iota(jnp.int32, (LANES, LANES), 0)
    ov_n = lax.broadcasted_iota(jnp.int32, (LANES, LANES), 1)
    ov_t = jnp.where(ov_s < n_slc,
                     jnp.where(ov_n * CMP_STRIDE < (ov_s + 1) * SLC_BLOCK,
                               jnp.where(ov_n * CMP_STRIDE + CMP_BLOCK > ov_s * SLC_BLOCK, 1.0, 0.0), 0.0),
                     0.0).astype(BF16)
    s_idx = lax.broadcasted_iota(jnp.int32, (n_slc, qb), 0)
    t_off = lax.broadcasted_iota(jnp.int32, (n_slc, qb), 1)
    pad = jnp.zeros((HEAD_DIM - n_slc, qb), F32)

    def qblock(i, carry):
        r0 = pl.multiple_of(i * qb, qb)
        q_c = _stack_heads(qc_ref[0, pl.ds(r0, qb), :], range(hp), lo_lane)
        valid, bias = _cmp_valid_bias(r0, qb)
        s = _dot_nt(q_c, kc_ref[0, 0]).reshape(hp, qb, LANES) + bias[None]
        e = jnp.exp2(s - jnp.max(s, axis=2, keepdims=True))
        p = jnp.where(valid[None], e / jnp.sum(e, axis=2, keepdims=True), 0.0)
        psum = jnp.sum(p, axis=0)
        imp_t = sum(_dot_nt(ov_t, term) for term in _split3(psum))
        blk_t = lax.shift_right_logical(r0 + t_off, SLC_SHIFT)
        causal = s_idx <= blk_t
        forced = (s_idx == 0) | (s_idx == blk_t) | (s_idx == blk_t - 1)
        score = jnp.where(forced, FORCE_SCORE, jnp.where(causal, imp_t[:n_slc], -1.0))
        rank = jnp.zeros((n_slc, qb), F32)
        for sp in range(n_slc):
            other = score[sp:sp + 1, :]
            tie = jnp.where(s_idx > sp, 1.0, 0.0)
            rank = rank + jnp.where(other > score, 1.0, jnp.where(other == score, tie, 0.0))
        sel_t = jnp.where(causal, jnp.where(rank < SLC_TOPN, 1.0, 0.0), 0.0)
        both = jnp.concatenate([sel_t, pad, sel_t, pad], axis=0).T
        o_ref[0, 0, pl.ds(r0, qb), :] = jnp.where(bias_lane, jnp.where(both > 0.5, 0.0, NEG_INF), 0.0).astype(BF16)
        return carry

    lax.fori_loop(0, t // qb, qblock, 0)


def _select(qc, kvc):
    b, t, _ = qc.shape
    n = t // CMP_STRIDE
    return pl.pallas_call(
        _sel_kernel,
        grid=(b, NSA_GROUPS),
        in_specs=[pl.BlockSpec((1, t, GROUP_W), lambda i, g: (i, 0, g)),
                  pl.BlockSpec((1, 1, n, LANES), lambda i, g: (i, g, 0, 0))],
        out_specs=pl.BlockSpec((1, 1, t, LANES), lambda i, g: (i, g, 0, 0)),
        out_shape=jax.ShapeDtypeStruct((b, NSA_GROUPS, t, LANES), BF16),
        compiler_params=_params("parallel", "parallel"),
        name="nsa_sel",
    )(qc, kvc)


def _nsa_kernel(qr_ref, qc_ref, kc_ref, vc_ref, ks_ref, vs_ref, kw_ref, vw_ref, gate_ref, sel_ref, o_ref,
                vsa_ref, vwa_ref, vca_ref, kse_ref, kso_ref):
    t = qr_ref.shape[1]
    qb = QB_NSA
    pair = pl.program_id(2)
    _fill_value_slab(vsa_ref, vs_ref[0])
    _fill_value_slab(vwa_ref, vw_ref[0])
    _fill_value_slab(vca_ref, vc_ref[0, 0])
    lane_t = lax.broadcasted_iota(jnp.int32, (t, LANES), 1)
    key_blk = lax.shift_right_logical(lax.broadcasted_iota(jnp.int32, (t, LANES), 0), SLC_SHIFT)
    onehot = jnp.where((lane_t & (HEAD_DIM - 1)) == key_blk, 1.0, 0.0).astype(BF16)
    ks2 = ks_ref[0]
    kse_ref[...] = jnp.where(lane_t < HEAD_DIM, ks2, onehot)
    kso_ref[...] = jnp.where(lane_t < HEAD_DIM, onehot, ks2)

    lane = lax.broadcasted_iota(jnp.int32, (qb, LANES), 1)
    lo_lane = lane < HEAD_DIM
    diag_bias = _causal_bias(qb)

    def win_bias(w0, r0):
        n = r0 + qb - w0
        kpos = w0 + lax.broadcasted_iota(jnp.int32, (qb, n), 1)
        qpos = r0 + lax.broadcasted_iota(jnp.int32, (qb, n), 0)
        return jnp.where(kpos <= qpos, jnp.where(kpos > qpos - WIN, 0.0, NEG_INF), NEG_INF)

    steady_win_bias = win_bias(0, WIN)

    def normalised(oa):
        return oa[:, :LANES] / oa[:, LANES:]

    for i in range(t // qb):
        r0 = i * qb
        rows = slice(r0, r0 + qb)
        qr = qr_ref[0, rows, :]
        zero = jnp.zeros_like(qr)
        valid, cbias = _cmp_valid_bias(r0, qb)
        s = _dot_nt(_stack_heads(qc_ref[0, rows, :], range(2), lo_lane), kc_ref[0, 0]).reshape(2, qb, LANES)
        s = s + cbias[None]
        e = jnp.where(valid[None], jnp.exp2(s - jnp.max(s, axis=2, keepdims=True)), 0.0)
        oa = _dot(e.reshape(2 * qb, LANES).astype(BF16), vca_ref[...])
        o_cmp = oa[:, :LANES] / jnp.maximum(oa[:, LANES:], 1.0)

        selb = sel_ref[0, 0, rows, :]
        q_e = jnp.where(lo_lane, qr, selb)
        q_o = jnp.where(lo_lane, selb, qr)
        sd = jnp.stack([_dot_nt(q_e, kse_ref[rows, :]), _dot_nt(q_o, kso_ref[rows, :])]) + diag_bias[None]
        m = jnp.max(sd, axis=2, keepdims=True)
        if i:
            sm = jnp.stack([_dot_nt(q_e, kse_ref[:r0, :]), _dot_nt(q_o, kso_ref[:r0, :])])
            m = jnp.maximum(m, jnp.max(sm, axis=2, keepdims=True))
            pm = jnp.exp2(sm - m).astype(BF16).reshape(2 * qb, r0)
        pd = jnp.exp2(sd - m).astype(BF16).reshape(2 * qb, qb)
        oa = _dot(pd, vsa_ref[rows, :])
        if i:
            oa = oa + _dot(pm, vsa_ref[:r0, :])
        o_slc = normalised(oa)

        w0 = max(r0 - WIN, 0)
        bias_w = steady_win_bias if r0 >= WIN else win_bias(w0, r0)
        q2 = jnp.concatenate([jnp.where(lo_lane, qr, zero), jnp.where(lo_lane, zero, qr)], axis=0)
        sw = _dot_nt(q2, kw_ref[0, w0:r0 + qb, :]).reshape(2, qb, r0 + qb - w0) + bias_w[None]
        pw = jnp.exp2(sw - jnp.max(sw, axis=2, keepdims=True)).astype(BF16).reshape(2 * qb, r0 + qb - w0)
        o_win = normalised(_dot(pw, vwa_ref[w0:r0 + qb, :]))

        gate = gate_ref[0, rows, :]
        outs = []
        for par in range(2):
            col = 6 * pair + 3 * par
            g = [jnp.sum(jnp.where(lane == col + br, gate, 0.0), axis=1, keepdims=True) for br in range(3)]
            hr = slice(par * qb, (par + 1) * qb)
            outs.append(g[0] * o_cmp[hr] + g[1] * o_slc[hr] + g[2] * o_win[hr])
        o_ref[0, rows, :] = jnp.where(lo_lane, outs[0], outs[1]).astype(o_ref.dtype)


def _nsa(qr, qc, kvc, ks2, vs2, kw2, vw2, gns, selb):
    b, t, _ = qr.shape
    n = t // CMP_STRIDE
    pair = pl.BlockSpec((1, t, LANES), lambda i, g, p: (i, 0, g * PAIRS + p))
    grp = pl.BlockSpec((1, t, LANES), lambda i, g, p: (i, 0, g))
    cmp_k = pl.BlockSpec((1, 1, n, LANES), lambda i, g, p: (i, g, 0, 0))
    cmp_v = pl.BlockSpec((1, 1, n, LANES), lambda i, g, p: (i, NSA_GROUPS + g, 0, 0))
    sel = pl.BlockSpec((1, 1, t, LANES), lambda i, g, p: (i, g, 0, 0))
    slab = lambda rows: pltpu.VMEM((rows, 2 * LANES), BF16)
    return pl.pallas_call(
        _nsa_kernel,
        grid=(b, NSA_GROUPS, PAIRS),
        in_specs=[pair, pair, cmp_k, cmp_v, grp, grp, grp, grp, grp, sel],
        out_specs=pair,
        out_shape=jax.ShapeDtypeStruct((b, t, NSA_W), BF16),
        scratch_shapes=[slab(t), slab(t), slab(n), pltpu.VMEM((t, LANES), BF16), pltpu.VMEM((t, LANES), BF16)],
        compiler_params=_params("parallel", "parallel", "parallel"),
        name="nsa_attn",
    )(qr, qc, kvc, kvc, ks2, vs2, kw2, vw2, gns, selb)


def _merge_kernel(x_ref, yda_ref, yns_ref, ga_ref, gb_ref, wa_ref, wb_ref, wo_ref, o_ref):
    merged = ga_ref[...] * _dot(yda_ref[...], wa_ref[...]) + gb_ref[...] * _dot(yns_ref[...], wb_ref[...])
    o_ref[...] = x_ref[...] + _dot(merged.astype(BF16), wo_ref[...])


def _merge(x2, yda, yns, ga, gb, w_da, w_nsa, w_out):
    n = x2.shape[0]
    row = pl.BlockSpec((TM_FFN, D_MODEL), lambda i: (i, 0))
    wspec = _resident((D_MODEL, D_MODEL))
    return pl.pallas_call(
        _merge_kernel,
        grid=(n // TM_FFN,),
        in_specs=[row] * 5 + [wspec] * 3,
        out_specs=row,
        out_shape=jax.ShapeDtypeStruct((n, D_MODEL), F32),
        compiler_params=_params("parallel"),
        name="merge",
    )(x2, yda, yns, ga, gb, w_da.astype(BF16), w_nsa.astype(BF16), w_out.astype(BF16))


def kernel(x, ffn1_norm, ffn1_w1, ffn1_w3, ffn1_w2, mix_norm, w_in, da_lambda, da_head_norm, cmp_pos, cmp_w1,
           cmp_w2, w_proj_da, w_proj_nsa, w_out, ffn2_norm, ffn2_w1, ffn2_w3, ffn2_w2, final_norm):
    b, t, d = x.shape
    depth = ffn1_norm.shape[0]
    if depth == 0:
        raise ValueError("depth must be positive")
    x2 = x.reshape(b * t, d)
    for l in range(depth):
        lam_init = 0.8 - 0.6 * math.exp(-0.3 * l)
        x2 = _ffn(x2, ffn1_norm[l], ffn1_w1[l], ffn1_w3[l], ffn1_w2[l])
        (qda, kda, vda, qnr, qnc, kcv, ks2, vs2, kw2, vw2, gns, ga, gb) = _inproj(
            x2.reshape(b, t, d), mix_norm[l], w_in[l])
        yda = _diff_attention(qda, kda, vda, da_lambda[l], da_head_norm[l], lam_init)
        kvc = _compress(kcv, cmp_pos[l], cmp_w1[l], cmp_w2[l])
        selb = _select(qnc, kvc)
        yns = _nsa(qnr, qnc, kvc, ks2, vs2, kw2, vw2, gns, selb)
        flat = lambda a: a.reshape(b * t, a.shape[-1])
        x2 = _merge(x2, flat(yda), flat(yns), flat(ga), flat(gb), w_proj_da[l], w_proj_nsa[l], w_out[l])
        last = l == depth - 1
        x2 = _ffn(x2, ffn2_norm[l], ffn2_w1[l], ffn2_w3[l], ffn2_w2[l], final_norm if last else None)
    return x2.reshape(b, t, d)
```

```python
import functools
import math

import jax
import jax.numpy as jnp
from jax import lax
from jax.experimental import pallas as pl
from jax.experimental.pallas import tpu as pltpu

D_MODEL = 1024
HEAD_DIM = 64
DA_HEADS = 8
NSA_HEADS = 16
NSA_GROUPS = 2
NSA_HPG = NSA_HEADS // NSA_GROUPS
CMP_BLOCK = 32
CMP_STRIDE = 16
CMP_HIDDEN = 256
SLC_BLOCK = 64
SLC_TOPN = 16
WIN = 512
D_FF = 2816
ROPE_THETA = 10000.0
EPS = 1e-6
NEG_INF = -1e30
FORCE_SCORE = 1e9
LOG2E = math.log2(math.e)

LANES = 128
DA_W = DA_HEADS * 2 * HEAD_DIM
NSA_W = NSA_HEADS * HEAD_DIM
GROUP_W = NSA_HPG * HEAD_DIM
PAIRS = NSA_HPG // 2
KV_W = NSA_GROUPS * HEAD_DIM
GATE_W = 3 * NSA_HEADS
GATE_PER_GROUP = 3 * NSA_HPG
SLC_SHIFT = SLC_BLOCK.bit_length() - 1
assert 1 << SLC_SHIFT == SLC_BLOCK

TM_FFN = 512
TM_PROJ = 256
FF_CHUNK = 256
QB_DA = 256
QB_NSA = 256
assert QB_NSA & (QB_NSA - 1) == 0
VMEM_LIMIT = 56 * 1024 * 1024

BF16 = jnp.bfloat16
F32 = jnp.float32


def _rms(x, g):
    return x * lax.rsqrt(jnp.mean(x * x, axis=-1, keepdims=True) + EPS) * g


def _dot(a, b):
    return jnp.dot(a, b, preferred_element_type=F32)


def _dot_nt(a, b):
    return lax.dot_general(a, b, (((1,), (1,)), ((), ())), preferred_element_type=F32)


def _resident(shape):
    nd = len(shape)
    return pl.BlockSpec(shape, lambda *_: (0,) * nd, pipeline_mode=pl.Buffered(1))


def _params(*sem):
    return pltpu.CompilerParams(dimension_semantics=sem, vmem_limit_bytes=VMEM_LIMIT)


def _swiglu_residual(x, g, w1_ref, w3_ref, w2_ref):
    h = _rms(x, g).astype(BF16)
    acc = jnp.zeros_like(x)
    for c in range(D_FF // FF_CHUNK):
        sl = slice(c * FF_CHUNK, (c + 1) * FF_CHUNK)
        a = _dot(h, w1_ref[:, sl])
        b = _dot(h, w3_ref[:, sl])
        u = (a * jax.nn.sigmoid(a) * b).astype(BF16)
        acc = acc + _dot(u, w2_ref[sl, :])
    return x + 0.5 * acc


def _ffn_kernel(x_ref, g_ref, w1_ref, w3_ref, w2_ref, o_ref):
    o_ref[...] = _swiglu_residual(x_ref[...], g_ref[...], w1_ref, w3_ref, w2_ref)


def _ffn_specs():
    return [_resident((1, D_MODEL)), _resident((D_MODEL, D_FF)), _resident((D_MODEL, D_FF)),
            _resident((D_FF, D_MODEL))]


def _ffn(x2, g, w1, w3, w2):
    n = x2.shape[0]
    row = pl.BlockSpec((TM_FFN, D_MODEL), lambda i: (i, 0))
    return pl.pallas_call(
        _ffn_kernel,
        grid=(n // TM_FFN,),
        in_specs=[row] + _ffn_specs(),
        out_specs=row,
        out_shape=jax.ShapeDtypeStruct((n, D_MODEL), F32),
        compiler_params=_params("parallel"),
        name="ffn",
    )(x2, g.reshape(1, D_MODEL), w1.astype(BF16), w3.astype(BF16), w2.astype(BF16))


_C_QDA, _C_KDA, _C_VDA, _C_QNS = 0, 1024, 2048, 3072
_C_KCV = 4096
_C_KS2 = 4352
_C_VS2 = 4608
_C_KW2 = 4864
_C_VW2 = 5120
_C_GNS = 5376
_C_GA = 5632
_C_GB = 6656
_MAIN_W = 7680


def _proj_kernel(x_ref, g_ref, w_ref, cos_ref, sa_ref, sb_ref,
                 qda_ref, kda_ref, vda_ref, qnr_ref, qnc_ref, kcv_ref,
                 ks2_ref, vs2_ref, kw2_ref, vw2_ref, gns_ref, ga_ref, gb_ref):
    h = _rms(x_ref[0], g_ref[...]).astype(BF16)
    cos, sa, sb = cos_ref[...], sa_ref[...], sb_ref[...]

    def proj(c0, width):
        return _dot(h, w_ref[:, c0:c0 + width])

    def rope(y):
        cols = []
        for c in range(y.shape[1] // LANES):
            v = y[:, c * LANES:(c + 1) * LANES]
            cols.append(v * cos + pltpu.roll(v, LANES - 32, 1) * sa + pltpu.roll(v, 32, 1) * sb)
        return cols[0] if len(cols) == 1 else jnp.concatenate(cols, axis=1)

    qscale = HEAD_DIM ** -0.5 * LOG2E
    half = DA_W // 2
    for c0 in (0, half):
        qda_ref[0, :, c0:c0 + half] = (rope(proj(_C_QDA + c0, half)) * qscale).astype(BF16)
        kda_ref[0, :, c0:c0 + half] = rope(proj(_C_KDA + c0, half)).astype(BF16)
        vda_ref[0, :, c0:c0 + half] = proj(_C_VDA + c0, half).astype(BF16)
        qn = proj(_C_QNS + c0, half)
        qnc_ref[0, :, c0:c0 + half] = (qn * qscale).astype(BF16)
        qnr_ref[0, :, c0:c0 + half] = (rope(qn) * qscale).astype(BF16)
        ga_ref[0, :, c0:c0 + half] = jax.nn.sigmoid(proj(_C_GA + c0, half))
        gb_ref[0, :, c0:c0 + half] = jax.nn.sigmoid(proj(_C_GB + c0, half))
    kcv_ref[0] = proj(_C_KCV, 256)
    ks2_ref[0] = rope(proj(_C_KS2, 256)).astype(BF16)
    vs2_ref[0] = proj(_C_VS2, 256).astype(BF16)
    kw2_ref[0] = rope(proj(_C_KW2, 256)).astype(BF16)
    vw2_ref[0] = proj(_C_VW2, 256).astype(BF16)
    gns_ref[0] = jax.nn.sigmoid(proj(_C_GNS, 256))


def _proj_weights(w_in):
    offs = [0]
    for w in (DA_W, DA_W, DA_W, NSA_W) + (KV_W,) * 6 + (GATE_W, D_MODEL, D_MODEL):
        offs.append(offs[-1] + w)
    (o_q, o_k, o_v, o_qn, o_kc, o_vc, o_ks, o_vs, o_kw, o_vw, o_g, o_ga, o_gb, _) = offs

    def dup(o):
        parts = []
        for g in range(NSA_GROUPS):
            blk = w_in[:, o + g * HEAD_DIM:o + (g + 1) * HEAD_DIM]
            parts += [blk, blk]
        return jnp.concatenate(parts, axis=1)

    gate_parts = []
    for g in range(NSA_GROUPS):
        gate_parts.append(jnp.pad(w_in[:, o_g + g * GATE_PER_GROUP:o_g + (g + 1) * GATE_PER_GROUP],
                                  ((0, 0), (0, LANES - GATE_PER_GROUP))))
    slab = jnp.concatenate(
        [w_in[:, o_q:o_kc], w_in[:, o_kc:o_ks], dup(o_ks), dup(o_vs), dup(o_kw), dup(o_vw)]
        + gate_parts + [w_in[:, o_ga:o_gb], w_in[:, o_gb:]], axis=1)
    return slab.astype(BF16)


def _rope_tables(t):
    pos = jnp.arange(t, dtype=F32)
    inv = 1.0 / (ROPE_THETA ** (jnp.arange(0, HEAD_DIM, 2, dtype=F32) / HEAD_DIM))
    ang = pos[:, None] * inv[None, :]
    cos, sin = jnp.cos(ang), jnp.sin(ang)
    zero = jnp.zeros_like(sin)
    cos128 = jnp.tile(cos, (1, 4))
    sa = jnp.tile(jnp.concatenate([-sin, zero], axis=1), (1, 2))
    sb = jnp.tile(jnp.concatenate([zero, sin], axis=1), (1, 2))
    return cos128, sa, sb


def _inproj(x3, g, w_in):
    b, t, _ = x3.shape
    cos, sa, sb = _rope_tables(t)
    tok = lambda w: pl.BlockSpec((1, TM_PROJ, w), lambda i, j: (i, j, 0))
    tab = pl.BlockSpec((TM_PROJ, LANES), lambda i, j: (j, 0))
    sds = lambda w, dt: jax.ShapeDtypeStruct((b, t, w), dt)
    outs = [(DA_W, BF16)] * 3 + [(NSA_W, BF16)] * 2 + [(256, F32)] + [(256, BF16)] * 4 \
        + [(256, F32), (D_MODEL, F32), (D_MODEL, F32)]
    return pl.pallas_call(
        _proj_kernel,
        grid=(b, t // TM_PROJ),
        in_specs=[tok(D_MODEL), _resident((1, D_MODEL)), _resident((D_MODEL, _MAIN_W)), tab, tab, tab],
        out_specs=[tok(w) for w, _ in outs],
        out_shape=[sds(w, dt) for w, dt in outs],
        compiler_params=_params("parallel", "parallel"),
        name="inproj",
    )(x3, g.reshape(1, D_MODEL), _proj_weights(w_in), cos, sa, sb)


def _fill_value_slab(va_ref, v):
    va_ref[:, :LANES] = v
    va_ref[:, LANES:] = jnp.ones(v.shape, v.dtype)


def _causal_bias(qb):
    tri = lax.broadcasted_iota(jnp.int32, (qb, qb), 1) <= lax.broadcasted_iota(jnp.int32, (qb, qb), 0)
    return jnp.where(tri, 0.0, NEG_INF)


def _da_kernel(lam_ref, gain_ref, q_ref, k_ref, v_ref, o_ref, va_ref, *, lam_init):
    t = q_ref.shape[1]
    qb = QB_DA
    _fill_value_slab(va_ref, v_ref[0])
    lp = lam_ref[...]
    lam = (jnp.exp(jnp.sum(lp[0:1] * lp[1:2], axis=1, keepdims=True))
           - jnp.exp(jnp.sum(lp[2:3] * lp[3:4], axis=1, keepdims=True)) + lam_init)
    gain = gain_ref[0] * (1.0 - lam_init)
    lane = lax.broadcasted_iota(jnp.int32, (qb, LANES), 1)
    diag_bias = _causal_bias(qb)

    for i in range(t // qb):
        r0 = i * qb
        q = q_ref[0, r0:r0 + qb, :]
        zero = jnp.zeros_like(q)
        qs = jnp.concatenate([jnp.where(lane < HEAD_DIM, q, zero), jnp.where(lane >= HEAD_DIM, q, zero)], axis=0)
        sd = _dot_nt(qs, k_ref[0, r0:r0 + qb, :]).reshape(2, qb, qb) + diag_bias[None]
        m = jnp.max(sd, axis=2, keepdims=True)
        if i:
            sm = _dot_nt(qs, k_ref[0, :r0, :]).reshape(2, qb, r0)
            m = jnp.maximum(m, jnp.max(sm, axis=2, keepdims=True))
            pm = jnp.exp2(sm - m).astype(BF16).reshape(2 * qb, r0)
        pd = jnp.exp2(sd - m).astype(BF16).reshape(2 * qb, qb)
        oa = _dot(pd, va_ref[r0:r0 + qb, :])
        if i:
            oa = oa + _dot(pm, va_ref[:r0, :])
        o = oa[:, :LANES] / oa[:, LANES:]
        o = o[:qb] - lam * o[qb:]
        o_ref[0, r0:r0 + qb, :] = (_rms(o, gain)).astype(o_ref.dtype)


def _diff_attention(q, k, v, da_lambda, head_gain, lam_init):
    b, t, _ = q.shape
    head = pl.BlockSpec((1, t, LANES), lambda i, h: (i, 0, h))
    return pl.pallas_call(
        functools.partial(_da_kernel, lam_init=lam_init),
        grid=(b, DA_HEADS),
        in_specs=[pl.BlockSpec((4, HEAD_DIM), lambda i, h: (0, 0)),
                  pl.BlockSpec((1, 1, LANES), lambda i, h: (h, 0, 0)), head, head, head],
        out_specs=head,
        out_shape=jax.ShapeDtypeStruct((b, t, DA_W), BF16),
        scratch_shapes=[pltpu.VMEM((t, 2 * LANES), BF16)],
        compiler_params=_params("parallel", "parallel"),
        name="diffattn",
    )(da_lambda, head_gain.reshape(DA_HEADS, 1, LANES), q, k, v)


def _cmp_kernel(x_ref, pos_ref, w1_ref, w2_ref, o_ref):
    half = CMP_STRIDE * HEAD_DIM
    for kv in range(2):
        for g in range(NSA_GROUPS):
            x = x_ref[0, kv * NSA_GROUPS + g]
            pos = pos_ref[kv]
            xa = (x + pos[:, :half]).astype(BF16)
            xb = (x + pos[:, half:]).astype(BF16)
            a = _dot(xa, w1_ref[kv, :half, :])
            bm = _dot(xb, w1_ref[kv, half:, :])
            hid = a + pltpu.roll(bm, bm.shape[0] - 1, 0)
            act = jax.nn.gelu(hid).astype(BF16)
            o_ref[0, kv * NSA_GROUPS + g] = _dot(act, w2_ref[kv]).astype(o_ref.dtype)


def _compress(kcv, cmp_pos, cmp_w1, cmp_w2):
    b, t, _ = kcv.shape
    n = t // CMP_STRIDE
    half = CMP_STRIDE * HEAD_DIM
    xc = kcv.reshape(b, n, CMP_STRIDE, 2 * NSA_GROUPS, HEAD_DIM).transpose(0, 3, 1, 2, 4).reshape(
        b, 2 * NSA_GROUPS, n, half)
    w2d = jnp.concatenate([cmp_w2, cmp_w2], axis=-1).astype(BF16)
    return pl.pallas_call(
        _cmp_kernel,
        grid=(b,),
        in_specs=[pl.BlockSpec((1, 2 * NSA_GROUPS, n, half), lambda i: (i, 0, 0, 0)),
                  pl.BlockSpec((2, 1, CMP_BLOCK * HEAD_DIM), lambda i: (0, 0, 0)),
                  pl.BlockSpec((2, CMP_BLOCK * HEAD_DIM, CMP_HIDDEN), lambda i: (0, 0, 0)),
                  pl.BlockSpec((2, CMP_HIDDEN, LANES), lambda i: (0, 0, 0))],
        out_specs=pl.BlockSpec((1, 2 * NSA_GROUPS, n, LANES), lambda i: (i, 0, 0, 0)),
        out_shape=jax.ShapeDtypeStruct((b, 2 * NSA_GROUPS, n, LANES), BF16),
        compiler_params=_params("parallel"),
        name="compress",
    )(xc, cmp_pos.reshape(2, 1, CMP_BLOCK * HEAD_DIM), cmp_w1.astype(BF16), w2d)


def _split3(x):
    hi = x.astype(BF16)
    r1 = x - hi.astype(F32)
    mid = r1.astype(BF16)
    lo = (r1 - mid.astype(F32)).astype(BF16)
    return hi, mid, lo


def _stack_heads(qblk, heads, lo_lane):
    zero = jnp.zeros((qblk.shape[0], LANES), qblk.dtype)
    parts = []
    for j in heads:
        pair = qblk[:, (j // 2) * LANES:(j // 2 + 1) * LANES]
        parts.append(jnp.where(lo_lane if j % 2 == 0 else ~lo_lane, pair, zero))
    return jnp.concatenate(parts, axis=0)


def _sel_kernel(qc_ref, kc_ref, vc_ref, sel_ref, ocmp_ref):
    t = qc_ref.shape[1]
    qb, hp = QB_NSA, NSA_HPG
    n_slc = t // SLC_BLOCK
    lane = lax.broadcasted_iota(jnp.int32, (qb, LANES), 1)
    lo_lane = lane < HEAD_DIM
    bias_lane = (lane & (HEAD_DIM - 1)) < n_slc
    ov_s = lax.broadcasted_iota(jnp.int32, (LANES, LANES), 0)
    ov_n = lax.broadcasted_iota(jnp.int32, (LANES, LANES), 1)
    ov_t = jnp.where(ov_s < n_slc,
                     jnp.where(ov_n * CMP_STRIDE < (ov_s + 1) * SLC_BLOCK,
                               jnp.where(ov_n * CMP_STRIDE + CMP_BLOCK > ov_s * SLC_BLOCK, 1.0, 0.0), 0.0),
                     0.0).astype(BF16)
    s_idx = lax.broadcasted_iota(jnp.int32, (n_slc, qb), 0)
    t_off = lax.broadcasted_iota(jnp.int32, (n_slc, qb), 1)
    pad = jnp.zeros((HEAD_DIM - n_slc, qb), F32)
    blk_n = lax.broadcasted_iota(jnp.int32, (LANES, hp * qb), 0)
    blk_q = lax.broadcasted_iota(jnp.int32, (LANES, hp * qb), 1) & (qb - 1)
    vc_t = vc_ref[0, 0].astype(F32).T[:HEAD_DIM].astype(BF16)

    def qblock(i, carry):
        r0 = pl.multiple_of(i * qb, qb)
        q_c = _stack_heads(qc_ref[0, pl.ds(r0, qb), :], range(hp), lo_lane)
        valid = (blk_n * CMP_STRIDE + (CMP_BLOCK - 1)) <= (r0 + blk_q)
        s = _dot_nt(kc_ref[0, 0], q_c) + jnp.where(valid, 0.0, NEG_INF)
        e = jnp.exp2(s - jnp.max(s, axis=0, keepdims=True))
        p = jnp.where(valid, e / jnp.sum(e, axis=0, keepdims=True), 0.0)
        o_t = _dot(vc_t, p.astype(BF16))
        for pr in range(hp // 2):
            both = jnp.concatenate([o_t[:, (2 * pr) * qb:(2 * pr + 1) * qb],
                                    o_t[:, (2 * pr + 1) * qb:(2 * pr + 2) * qb]], axis=0)
            ocmp_ref[0, pl.ds(r0, qb), pr * LANES:(pr + 1) * LANES] = both.T
        psum = sum(p[:, j * qb:(j + 1) * qb] for j in range(hp))
        imp_t = sum(_dot(ov_t, term) for term in _split3(psum))
        blk_t = lax.shift_right_logical(r0 + t_off, SLC_SHIFT)
        causal = s_idx <= blk_t
        forced = (s_idx == 0) | (s_idx == blk_t) | (s_idx == blk_t - 1)
        score = jnp.where(forced, FORCE_SCORE, jnp.where(causal, imp_t[:n_slc], -1.0))
        rank = jnp.zeros((n_slc, qb), F32)
        for sp in range(n_slc):
            other = score[sp:sp + 1, :]
            tie = jnp.where(s_idx > sp, 1.0, 0.0)
            rank = rank + jnp.where(other > score, 1.0, jnp.where(other == score, tie, 0.0))
        sel_t = jnp.where(causal, jnp.where(rank < SLC_TOPN, 1.0, 0.0), 0.0)
        both = jnp.concatenate([sel_t, pad, sel_t, pad], axis=0).T
        sel_ref[0, 0, pl.ds(r0, qb), :] = jnp.where(bias_lane, jnp.where(both > 0.5, 0.0, NEG_INF), 0.0).astype(BF16)
        return carry

    lax.fori_loop(0, t // qb, qblock, 0)


def _select(qc, kvc):
    b, t, _ = qc.shape
    n = t // CMP_STRIDE
    grp = pl.BlockSpec((1, t, GROUP_W), lambda i, g: (i, 0, g))
    return pl.pallas_call(
        _sel_kernel,
        grid=(b, NSA_GROUPS),
        in_specs=[grp, pl.BlockSpec((1, 1, n, LANES), lambda i, g: (i, g, 0, 0)),
                  pl.BlockSpec((1, 1, n, LANES), lambda i, g: (i, NSA_GROUPS + g, 0, 0))],
        out_specs=[pl.BlockSpec((1, 1, t, LANES), lambda i, g: (i, g, 0, 0)), grp],
        out_shape=[jax.ShapeDtypeStruct((b, NSA_GROUPS, t, LANES), BF16),
                   jax.ShapeDtypeStruct((b, t, NSA_W), F32)],
        compiler_params=_params("parallel", "parallel"),
        name="nsa_sel",
    )(qc, kvc, kvc)


def _nsa_kernel(qr_ref, ocmp_ref, ks_ref, vs_ref, kw_ref, vw_ref, gate_ref, sel_ref, o_ref,
                vsa_ref, vwa_ref, kse_ref, kso_ref):
    t = qr_ref.shape[1]
    qb = QB_NSA
    pair = pl.program_id(2)
    _fill_value_slab(vsa_ref, vs_ref[0])
    _fill_value_slab(vwa_ref, vw_ref[0])
    lane_t = lax.broadcasted_iota(jnp.int32, (t, LANES), 1)
    key_blk = lax.shift_right_logical(lax.broadcasted_iota(jnp.int32, (t, LANES), 0), SLC_SHIFT)
    onehot = jnp.where((lane_t & (HEAD_DIM - 1)) == key_blk, 1.0, 0.0).astype(BF16)
    ks2 = ks_ref[0]
    kse_ref[...] = jnp.where(lane_t < HEAD_DIM, ks2, onehot)
    kso_ref[...] = jnp.where(lane_t < HEAD_DIM, onehot, ks2)

    lane = lax.broadcasted_iota(jnp.int32, (qb, LANES), 1)
    lo_lane = lane < HEAD_DIM
    diag_bias = _causal_bias(qb)

    def win_bias(w0, r0):
        n = r0 + qb - w0
        kpos = w0 + lax.broadcasted_iota(jnp.int32, (qb, n), 1)
        qpos = r0 + lax.broadcasted_iota(jnp.int32, (qb, n), 0)
        return jnp.where(kpos <= qpos, jnp.where(kpos > qpos - WIN, 0.0, NEG_INF), NEG_INF)

    steady_win_bias = win_bias(0, WIN)

    def normalised(oa):
        return oa[:, :LANES] / oa[:, LANES:]

    for i in range(t // qb):
        r0 = i * qb
        rows = slice(r0, r0 + qb)
        qr = qr_ref[0, rows, :]
        zero = jnp.zeros_like(qr)
        selb = sel_ref[0, 0, rows, :]
        q_e = jnp.where(lo_lane, qr, selb)
        q_o = jnp.where(lo_lane, selb, qr)
        sd = jnp.stack([_dot_nt(q_e, kse_ref[rows, :]), _dot_nt(q_o, kso_ref[rows, :])]) + diag_bias[None]
        m = jnp.max(sd, axis=2, keepdims=True)
        if i:
            sm = jnp.stack([_dot_nt(q_e, kse_ref[:r0, :]), _dot_nt(q_o, kso_ref[:r0, :])])
            m = jnp.maximum(m, jnp.max(sm, axis=2, keepdims=True))
            pm = jnp.exp2(sm - m).astype(BF16).reshape(2 * qb, r0)
        pd = jnp.exp2(sd - m).astype(BF16).reshape(2 * qb, qb)
        oa = _dot(pd, vsa_ref[rows, :])
        if i:
            oa = oa + _dot(pm, vsa_ref[:r0, :])
        o_slc = normalised(oa)

        w0 = max(r0 - WIN, 0)
        bias_w = steady_win_bias if r0 >= WIN else win_bias(w0, r0)
        q2 = jnp.concatenate([jnp.where(lo_lane, qr, zero), jnp.where(lo_lane, zero, qr)], axis=0)
        sw = _dot_nt(q2, kw_ref[0, w0:r0 + qb, :]).reshape(2, qb, r0 + qb - w0) + bias_w[None]
        pw = jnp.exp2(sw - jnp.max(sw, axis=2, keepdims=True)).astype(BF16).reshape(2 * qb, r0 + qb - w0)
        o_win = normalised(_dot(pw, vwa_ref[w0:r0 + qb, :]))

        gate = gate_ref[0, rows, :]
        o_cmp = ocmp_ref[0, rows, :]
        outs = []
        for par in range(2):
            col = 6 * pair + 3 * par
            g = [jnp.sum(jnp.where(lane == col + br, gate, 0.0), axis=1, keepdims=True) for br in range(3)]
            hr = slice(par * qb, (par + 1) * qb)
            outs.append(g[0] * o_cmp + g[1] * o_slc[hr] + g[2] * o_win[hr])
        o_ref[0, rows, :] = jnp.where(lo_lane, outs[0], outs[1]).astype(o_ref.dtype)


def _nsa(qr, ocmp, ks2, vs2, kw2, vw2, gns, selb):
    b, t, _ = qr.shape
    pair = pl.BlockSpec((1, t, LANES), lambda i, g, p: (i, 0, g * PAIRS + p))
    grp = pl.BlockSpec((1, t, LANES), lambda i, g, p: (i, 0, g))
    sel = pl.BlockSpec((1, 1, t, LANES), lambda i, g, p: (i, g, 0, 0))
    slab = pltpu.VMEM((t, 2 * LANES), BF16)
    keys = pltpu.VMEM((t, LANES), BF16)
    return pl.pallas_call(
        _nsa_kernel,
        grid=(b, NSA_GROUPS, PAIRS),
        in_specs=[pair, pair, grp, grp, grp, grp, grp, sel],
        out_specs=pair,
        out_shape=jax.ShapeDtypeStruct((b, t, NSA_W), BF16),
        scratch_shapes=[slab, slab, keys, keys],
        compiler_params=_params("parallel", "parallel", "parallel"),
        name="nsa_attn",
    )(qr, ocmp, ks2, vs2, kw2, vw2, gns, selb)


def _merge_ffn_kernel(x_ref, yda_ref, yns_ref, ga_ref, gb_ref, wa_ref, wb_ref, wo_ref,
                      g_ref, w1_ref, w3_ref, w2_ref, fg_ref, o_ref):
    merged = ga_ref[...] * _dot(yda_ref[...], wa_ref[...]) + gb_ref[...] * _dot(yns_ref[...], wb_ref[...])
    x = x_ref[...] + _dot(merged.astype(BF16), wo_ref[...])
    o_ref[...] = _rms(_swiglu_residual(x, g_ref[...], w1_ref, w3_ref, w2_ref), fg_ref[...])


def _merge_ffn(x2, yda, yns, ga, gb, w_da, w_nsa, w_out, g, w1, w3, w2, final_g):
    n = x2.shape[0]
    row = pl.BlockSpec((TM_FFN, D_MODEL), lambda i: (i, 0))
    wspec = _resident((D_MODEL, D_MODEL))
    return pl.pallas_call(
        _merge_ffn_kernel,
        grid=(n // TM_FFN,),
        in_specs=[row] * 5 + [wspec] * 3 + _ffn_specs() + [_resident((1, D_MODEL))],
        out_specs=row,
        out_shape=jax.ShapeDtypeStruct((n, D_MODEL), F32),
        compiler_params=_params("parallel"),
        name="merge_ffn",
    )(x2, yda, yns, ga, gb, w_da.astype(BF16), w_nsa.astype(BF16), w_out.astype(BF16),
      g.reshape(1, D_MODEL), w1.astype(BF16), w3.astype(BF16), w2.astype(BF16), final_g.reshape(1, D_MODEL))


def kernel(x, ffn1_norm, ffn1_w1, ffn1_w3, ffn1_w2, mix_norm, w_in, da_lambda, da_head_norm, cmp_pos, cmp_w1,
           cmp_w2, w_proj_da, w_proj_nsa, w_out, ffn2_norm, ffn2_w1, ffn2_w3, ffn2_w2, final_norm):
    b, t, d = x.shape
    depth = ffn1_norm.shape[0]
    assert depth == 1, "one decoder layer per call"
    lam_init = 0.8 - 0.6 * math.exp(-0.3 * 0)
    x2 = _ffn(x.reshape(b * t, d), ffn1_norm[0], ffn1_w1[0], ffn1_w3[0], ffn1_w2[0])
    (qda, kda, vda, qnr, qnc, kcv, ks2, vs2, kw2, vw2, gns, ga, gb) = _inproj(
        x2.reshape(b, t, d), mix_norm[0], w_in[0])
    yda = _diff_attention(qda, kda, vda, da_lambda[0], da_head_norm[0], lam_init)
    kvc = _compress(kcv, cmp_pos[0], cmp_w1[0], cmp_w2[0])
    selb, ocmp = _select(qnc, kvc)
    yns = _nsa(qnr, ocmp, ks2, vs2, kw2, vw2, gns, selb)
    flat = lambda a: a.reshape(b * t, a.shape[-1])
    out = _merge_ffn(x2, flat(yda), flat(yns), flat(ga), flat(gb), w_proj_da[0], w_proj_nsa[0], w_out[0],
                     ffn2_norm[0], ffn2_w1[0], ffn2_w3[0], ffn2_w2[0], final_norm)
    return out.reshape(b, t, d)
```

```python
import functools
import math

import jax
import jax.numpy as jnp
from jax import lax
from jax.experimental import pallas as pl
from jax.experimental.pallas import tpu as pltpu

D_MODEL = 1024
HEAD_DIM = 64
DA_HEADS = 8
NSA_HEADS = 16
NSA_GROUPS = 2
NSA_HPG = NSA_HEADS // NSA_GROUPS
CMP_BLOCK = 32
CMP_STRIDE = 16
CMP_HIDDEN = 256
SLC_BLOCK = 64
SLC_TOPN = 16
WIN = 512
D_FF = 2816
ROPE_THETA = 10000.0
EPS = 1e-6
NEG_INF = -1e30
FORCE_SCORE = 1e9
LOG2E = math.log2(math.e)

LANES = 128
DA_W = DA_HEADS * 2 * HEAD_DIM
NSA_W = NSA_HEADS * HEAD_DIM
GROUP_W = NSA_HPG * HEAD_DIM
PAIRS = NSA_HPG // 2
KV_W = NSA_GROUPS * HEAD_DIM
GATE_W = 3 * NSA_HEADS
GATE_PER_GROUP = 3 * NSA_HPG
SLC_SHIFT = SLC_BLOCK.bit_length() - 1
assert 1 << SLC_SHIFT == SLC_BLOCK

TM_FFN = 512
TM_PROJ = 256
FF_CHUNK = 256
QB_DA = 256
QB_NSA = 256
assert QB_NSA & (QB_NSA - 1) == 0
VMEM_LIMIT = 56 * 1024 * 1024

BF16 = jnp.bfloat16
F32 = jnp.float32


def _rms(x, g):
    return x * lax.rsqrt(jnp.mean(x * x, axis=-1, keepdims=True) + EPS) * g


def _dot(a, b):
    return jnp.dot(a, b, preferred_element_type=F32)


def _dot_nt(a, b):
    return lax.dot_general(a, b, (((1,), (1,)), ((), ())), preferred_element_type=F32)


def _resident(shape):
    nd = len(shape)
    return pl.BlockSpec(shape, lambda *_: (0,) * nd, pipeline_mode=pl.Buffered(1))


def _params(*sem):
    return pltpu.CompilerParams(dimension_semantics=sem, vmem_limit_bytes=VMEM_LIMIT)


def _swiglu_residual(x, g, w1_ref, w3_ref, w2_ref):
    h = _rms(x, g).astype(BF16)
    acc = jnp.zeros_like(x)
    for c in range(D_FF // FF_CHUNK):
        sl = slice(c * FF_CHUNK, (c + 1) * FF_CHUNK)
        a = _dot(h, w1_ref[:, sl])
        b = _dot(h, w3_ref[:, sl])
        u = (a * jax.nn.sigmoid(a) * b).astype(BF16)
        acc = acc + _dot(u, w2_ref[sl, :])
    return x + 0.5 * acc


def _ffn_kernel(x_ref, g_ref, w1_ref, w3_ref, w2_ref, o_ref):
    o_ref[...] = _swiglu_residual(x_ref[...], g_ref[...], w1_ref, w3_ref, w2_ref)


def _ffn_specs():
    return [_resident((1, D_MODEL)), _resident((D_MODEL, D_FF)), _resident((D_MODEL, D_FF)),
            _resident((D_FF, D_MODEL))]


def _ffn(x2, g, w1, w3, w2):
    n = x2.shape[0]
    row = pl.BlockSpec((TM_FFN, D_MODEL), lambda i: (i, 0))
    return pl.pallas_call(
        _ffn_kernel,
        grid=(n // TM_FFN,),
        in_specs=[row] + _ffn_specs(),
        out_specs=row,
        out_shape=jax.ShapeDtypeStruct((n, D_MODEL), F32),
        compiler_params=_params("parallel"),
        name="ffn",
    )(x2, g.reshape(1, D_MODEL), w1.astype(BF16), w3.astype(BF16), w2.astype(BF16))


_C_QDA, _C_KDA, _C_VDA, _C_QNS = 0, 1024, 2048, 3072
_C_KCV = 4096
_C_KS2 = 4352
_C_VS2 = 4608
_C_KW2 = 4864
_C_VW2 = 5120
_C_GNS = 5376
_C_GA = 5632
_C_GB = 6656
_MAIN_W = 7680


def _proj_kernel(x_ref, g_ref, w_ref, cos_ref, sa_ref, sb_ref,
                 qda_ref, kda_ref, vda_ref, qnr_ref, qnc_ref, kcv_ref,
                 ks2_ref, vs2_ref, kw2_ref, vw2_ref, gns_ref, ga_ref, gb_ref):
    h = _rms(x_ref[0], g_ref[...]).astype(BF16)
    cos, sa, sb = cos_ref[...], sa_ref[...], sb_ref[...]

    def proj(c0, width):
        return _dot(h, w_ref[:, c0:c0 + width])

    def rope(y):
        cols = []
        for c in range(y.shape[1] // LANES):
            v = y[:, c * LANES:(c + 1) * LANES]
            cols.append(v * cos + pltpu.roll(v, LANES - 32, 1) * sa + pltpu.roll(v, 32, 1) * sb)
        return cols[0] if len(cols) == 1 else jnp.concatenate(cols, axis=1)

    qscale = HEAD_DIM ** -0.5 * LOG2E
    half = DA_W // 2
    for c0 in (0, half):
        qda_ref[0, :, c0:c0 + half] = (rope(proj(_C_QDA + c0, half)) * qscale).astype(BF16)
        kda_ref[0, :, c0:c0 + half] = rope(proj(_C_KDA + c0, half)).astype(BF16)
        vda_ref[0, :, c0:c0 + half] = proj(_C_VDA + c0, half).astype(BF16)
        qn = proj(_C_QNS + c0, half)
        qnc_ref[0, :, c0:c0 + half] = (qn * qscale).astype(BF16)
        qnr_ref[0, :, c0:c0 + half] = (rope(qn) * qscale).astype(BF16)
        ga_ref[0, :, c0:c0 + half] = jax.nn.sigmoid(proj(_C_GA + c0, half))
        gb_ref[0, :, c0:c0 + half] = jax.nn.sigmoid(proj(_C_GB + c0, half))
    kcv_ref[0] = proj(_C_KCV, 256)
    ks2_ref[0] = rope(proj(_C_KS2, 256)).astype(BF16)
    vs2_ref[0] = proj(_C_VS2, 256).astype(BF16)
    kw2_ref[0] = rope(proj(_C_KW2, 256)).astype(BF16)
    vw2_ref[0] = proj(_C_VW2, 256).astype(BF16)
    gns_ref[0] = jax.nn.sigmoid(proj(_C_GNS, 256))


def _proj_weights(w_in):
    offs = [0]
    for w in (DA_W, DA_W, DA_W, NSA_W) + (KV_W,) * 6 + (GATE_W, D_MODEL, D_MODEL):
        offs.append(offs[-1] + w)
    (o_q, o_k, o_v, o_qn, o_kc, o_vc, o_ks, o_vs, o_kw, o_vw, o_g, o_ga, o_gb, _) = offs

    def dup(o):
        parts = []
        for g in range(NSA_GROUPS):
            blk = w_in[:, o + g * HEAD_DIM:o + (g + 1) * HEAD_DIM]
            parts += [blk, blk]
        return jnp.concatenate(parts, axis=1)

    gate_parts = []
    for g in range(NSA_GROUPS):
        gate_parts.append(jnp.pad(w_in[:, o_g + g * GATE_PER_GROUP:o_g + (g + 1) * GATE_PER_GROUP],
                                  ((0, 0), (0, LANES - GATE_PER_GROUP))))
    slab = jnp.concatenate(
        [w_in[:, o_q:o_kc], w_in[:, o_kc:o_ks], dup(o_ks), dup(o_vs), dup(o_kw), dup(o_vw)]
        + gate_parts + [w_in[:, o_ga:o_gb], w_in[:, o_gb:]], axis=1)
    return slab.astype(BF16)


def _rope_tables(t):
    pos = jnp.arange(t, dtype=F32)
    inv = 1.0 / (ROPE_THETA ** (jnp.arange(0, HEAD_DIM, 2, dtype=F32) / HEAD_DIM))
    ang = pos[:, None] * inv[None, :]
    cos, sin = jnp.cos(ang), jnp.sin(ang)
    zero = jnp.zeros_like(sin)
    cos128 = jnp.tile(cos, (1, 4))
    sa = jnp.tile(jnp.concatenate([-sin, zero], axis=1), (1, 2))
    sb = jnp.tile(jnp.concatenate([zero, sin], axis=1), (1, 2))
    return cos128, sa, sb


def _inproj(x3, g, w_in):
    b, t, _ = x3.shape
    cos, sa, sb = _rope_tables(t)
    tok = lambda w: pl.BlockSpec((1, TM_PROJ, w), lambda i, j: (i, j, 0))
    tab = pl.BlockSpec((TM_PROJ, LANES), lambda i, j: (j, 0))
    sds = lambda w, dt: jax.ShapeDtypeStruct((b, t, w), dt)
    outs = [(DA_W, BF16)] * 3 + [(NSA_W, BF16)] * 2 + [(256, F32)] + [(256, BF16)] * 4 \
        + [(256, F32), (D_MODEL, F32), (D_MODEL, F32)]
    return pl.pallas_call(
        _proj_kernel,
        grid=(b, t // TM_PROJ),
        in_specs=[tok(D_MODEL), _resident((1, D_MODEL)), _resident((D_MODEL, _MAIN_W)), tab, tab, tab],
        out_specs=[tok(w) for w, _ in outs],
        out_shape=[sds(w, dt) for w, dt in outs],
        compiler_params=_params("parallel", "parallel"),
        name="inproj",
    )(x3, g.reshape(1, D_MODEL), _proj_weights(w_in), cos, sa, sb)


def _fill_value_slab(va_ref, v):
    va_ref[:, :LANES] = v
    va_ref[:, LANES:] = jnp.ones(v.shape, v.dtype)


def _causal_bias(qb):
    tri = lax.broadcasted_iota(jnp.int32, (qb, qb), 1) <= lax.broadcasted_iota(jnp.int32, (qb, qb), 0)
    return jnp.where(tri, 0.0, NEG_INF)


def _causal_probs(s, diag_bias):
    qb = s.shape[1]
    r0 = s.shape[2] - qb
    sd = s[:, :, r0:] + diag_bias[None]
    m = jnp.max(sd, axis=2, keepdims=True)
    if not r0:
        return jnp.exp2(sd - m).astype(BF16)
    sm = s[:, :, :r0]
    m = jnp.maximum(m, jnp.max(sm, axis=2, keepdims=True))
    return jnp.concatenate([jnp.exp2(sm - m).astype(BF16), jnp.exp2(sd - m).astype(BF16)], axis=2)


def _da_kernel(lam_ref, gain_ref, q_ref, k_ref, v_ref, o_ref, va_ref, *, lam_init):
    t = q_ref.shape[1]
    qb = QB_DA
    _fill_value_slab(va_ref, v_ref[0])
    lp = lam_ref[...]
    lam = (jnp.exp(jnp.sum(lp[0:1] * lp[1:2], axis=1, keepdims=True))
           - jnp.exp(jnp.sum(lp[2:3] * lp[3:4], axis=1, keepdims=True)) + lam_init)
    gain = gain_ref[0] * (1.0 - lam_init)
    lane = lax.broadcasted_iota(jnp.int32, (qb, LANES), 1)
    diag_bias = _causal_bias(qb)

    for i in reversed(range(t // qb)):
        r0 = i * qb
        q = q_ref[0, r0:r0 + qb, :]
        zero = jnp.zeros_like(q)
        qs = jnp.concatenate([jnp.where(lane < HEAD_DIM, q, zero), jnp.where(lane >= HEAD_DIM, q, zero)], axis=0)
        p = _causal_probs(_dot_nt(qs, k_ref[0, :r0 + qb, :]).reshape(2, qb, r0 + qb), diag_bias)
        oa = _dot(p.reshape(2 * qb, r0 + qb), va_ref[:r0 + qb, :])
        o = oa[:, :LANES] / oa[:, LANES:]
        o = o[:qb] - lam * o[qb:]
        o_ref[0, r0:r0 + qb, :] = (_rms(o, gain)).astype(o_ref.dtype)


def _diff_attention(q, k, v, da_lambda, head_gain, lam_init):
    b, t, _ = q.shape
    head = pl.BlockSpec((1, t, LANES), lambda i, h: (i, 0, h))
    return pl.pallas_call(
        functools.partial(_da_kernel, lam_init=lam_init),
        grid=(b, DA_HEADS),
        in_specs=[pl.BlockSpec((4, HEAD_DIM), lambda i, h: (0, 0)),
                  pl.BlockSpec((1, 1, LANES), lambda i, h: (h, 0, 0)), head, head, head],
        out_specs=head,
        out_shape=jax.ShapeDtypeStruct((b, t, DA_W), BF16),
        scratch_shapes=[pltpu.VMEM((t, 2 * LANES), BF16)],
        compiler_params=_params("parallel", "parallel"),
        name="diffattn",
    )(da_lambda, head_gain.reshape(DA_HEADS, 1, LANES), q, k, v)


def _cmp_kernel(x_ref, pos_ref, w1_ref, w2_ref, o_ref):
    half = CMP_STRIDE * HEAD_DIM
    for kv in range(2):
        for g in range(NSA_GROUPS):
            x = x_ref[0, kv * NSA_GROUPS + g]
            pos = pos_ref[kv]
            xa = (x + pos[:, :half]).astype(BF16)
            xb = (x + pos[:, half:]).astype(BF16)
            a = _dot(xa, w1_ref[kv, :half, :])
            bm = _dot(xb, w1_ref[kv, half:, :])
            hid = a + pltpu.roll(bm, bm.shape[0] - 1, 0)
            act = jax.nn.gelu(hid).astype(BF16)
            o_ref[0, kv * NSA_GROUPS + g] = _dot(act, w2_ref[kv]).astype(o_ref.dtype)


def _compress(kcv, cmp_pos, cmp_w1, cmp_w2):
    b, t, _ = kcv.shape
    n = t // CMP_STRIDE
    half = CMP_STRIDE * HEAD_DIM
    xc = kcv.reshape(b, n, CMP_STRIDE, 2 * NSA_GROUPS, HEAD_DIM).transpose(0, 3, 1, 2, 4).reshape(
        b, 2 * NSA_GROUPS, n, half)
    w2d = jnp.concatenate([cmp_w2, cmp_w2], axis=-1).astype(BF16)
    return pl.pallas_call(
        _cmp_kernel,
        grid=(b,),
        in_specs=[pl.BlockSpec((1, 2 * NSA_GROUPS, n, half), lambda i: (i, 0, 0, 0)),
                  pl.BlockSpec((2, 1, CMP_BLOCK * HEAD_DIM), lambda i: (0, 0, 0)),
                  pl.BlockSpec((2, CMP_BLOCK * HEAD_DIM, CMP_HIDDEN), lambda i: (0, 0, 0)),
                  pl.BlockSpec((2, CMP_HIDDEN, LANES), lambda i: (0, 0, 0))],
        out_specs=pl.BlockSpec((1, 2 * NSA_GROUPS, n, LANES), lambda i: (i, 0, 0, 0)),
        out_shape=jax.ShapeDtypeStruct((b, 2 * NSA_GROUPS, n, LANES), BF16),
        compiler_params=_params("parallel"),
        name="compress",
    )(xc, cmp_pos.reshape(2, 1, CMP_BLOCK * HEAD_DIM), cmp_w1.astype(BF16), w2d)


def _split3(x):
    hi = x.astype(BF16)
    r1 = x - hi.astype(F32)
    mid = r1.astype(BF16)
    lo = (r1 - mid.astype(F32)).astype(BF16)
    return hi, mid, lo


def _stack_heads(qblk, heads, lo_lane):
    zero = jnp.zeros((qblk.shape[0], LANES), qblk.dtype)
    parts = []
    for j in heads:
        pair = qblk[:, (j // 2) * LANES:(j // 2 + 1) * LANES]
        parts.append(jnp.where(lo_lane if j % 2 == 0 else ~lo_lane, pair, zero))
    return jnp.concatenate(parts, axis=0)


def _sel_kernel(qc_ref, kc_ref, vc_ref, sel_ref, ocmp_ref):
    t = qc_ref.shape[1]
    qb, hp = QB_NSA, NSA_HPG
    n_slc = t // SLC_BLOCK
    lane = lax.broadcasted_iota(jnp.int32, (qb, LANES), 1)
    lo_lane = lane < HEAD_DIM
    bias_lane = (lane & (HEAD_DIM - 1)) < n_slc
    ov_s = lax.broadcasted_iota(jnp.int32, (LANES, LANES), 0)
    ov_n = lax.broadcasted_iota(jnp.int32, (LANES, LANES), 1)
    ov_t = jnp.where(ov_s < n_slc,
                     jnp.where(ov_n * CMP_STRIDE < (ov_s + 1) * SLC_BLOCK,
                               jnp.where(ov_n * CMP_STRIDE + CMP_BLOCK > ov_s * SLC_BLOCK, 1.0, 0.0), 0.0),
                     0.0).astype(BF16)
    s_idx = lax.broadcasted_iota(jnp.int32, (n_slc, qb), 0)
    t_off = lax.broadcasted_iota(jnp.int32, (n_slc, qb), 1)
    pad = jnp.zeros((HEAD_DIM - n_slc, qb), F32)
    blk_n = lax.broadcasted_iota(jnp.int32, (LANES, hp * qb), 0)
    blk_q = lax.broadcasted_iota(jnp.int32, (LANES, hp * qb), 1) & (qb - 1)
    vc_t = vc_ref[0, 0].astype(F32).T[:HEAD_DIM].astype(BF16)

    def qblock(i, carry):
        r0 = pl.multiple_of(i * qb, qb)
        q_c = _stack_heads(qc_ref[0, pl.ds(r0, qb), :], range(hp), lo_lane)
        valid = (blk_n * CMP_STRIDE + (CMP_BLOCK - 1)) <= (r0 + blk_q)
        s = _dot_nt(kc_ref[0, 0], q_c) + jnp.where(valid, 0.0, NEG_INF)
        e = jnp.exp2(s - jnp.max(s, axis=0, keepdims=True))
        p = jnp.where(valid, e / jnp.sum(e, axis=0, keepdims=True), 0.0)
        o_t = _dot(vc_t, p.astype(BF16))
        for pr in range(hp // 2):
            both = jnp.concatenate([o_t[:, (2 * pr) * qb:(2 * pr + 1) * qb],
                                    o_t[:, (2 * pr + 1) * qb:(2 * pr + 2) * qb]], axis=0)
            ocmp_ref[0, pl.ds(r0, qb), pr * LANES:(pr + 1) * LANES] = both.T
        psum = sum(p[:, j * qb:(j + 1) * qb] for j in range(hp))
        imp_t = sum(_dot(ov_t, term) for term in _split3(psum))
        blk_t = lax.shift_right_logical(r0 + t_off, SLC_SHIFT)
        causal = s_idx <= blk_t
        forced = (s_idx == 0) | (s_idx == blk_t) | (s_idx == blk_t - 1)
        score = jnp.where(forced, FORCE_SCORE, jnp.where(causal, imp_t[:n_slc], -1.0))
        rank = jnp.zeros((n_slc, qb), F32)
        for sp in range(n_slc):
            other = score[sp:sp + 1, :]
            tie = jnp.where(s_idx > sp, 1.0, 0.0)
            rank = rank + jnp.where(other > score, 1.0, jnp.where(other == score, tie, 0.0))
        sel_t = jnp.where(causal, jnp.where(rank < SLC_TOPN, 1.0, 0.0), 0.0)
        both = jnp.concatenate([sel_t, pad, sel_t, pad], axis=0).T
        sel_ref[0, 0, pl.ds(r0, qb), :] = jnp.where(bias_lane, jnp.where(both > 0.5, 0.0, NEG_INF), 0.0).astype(BF16)
        return carry

    lax.fori_loop(0, t // qb, qblock, 0)


def _select(qc, kvc):
    b, t, _ = qc.shape
    n = t // CMP_STRIDE
    grp = pl.BlockSpec((1, t, GROUP_W), lambda i, g: (i, 0, g))
    return pl.pallas_call(
        _sel_kernel,
        grid=(b, NSA_GROUPS),
        in_specs=[grp, pl.BlockSpec((1, 1, n, LANES), lambda i, g: (i, g, 0, 0)),
                  pl.BlockSpec((1, 1, n, LANES), lambda i, g: (i, NSA_GROUPS + g, 0, 0))],
        out_specs=[pl.BlockSpec((1, 1, t, LANES), lambda i, g: (i, g, 0, 0)), grp],
        out_shape=[jax.ShapeDtypeStruct((b, NSA_GROUPS, t, LANES), BF16),
                   jax.ShapeDtypeStruct((b, t, NSA_W), F32)],
        compiler_params=_params("parallel", "parallel"),
        name="nsa_sel",
    )(qc, kvc, kvc)


def _nsa_kernel(qr_ref, ocmp_ref, ks_ref, vs_ref, kw_ref, vw_ref, gate_ref, sel_ref, o_ref,
                vsa_ref, vwa_ref, kse_ref):
    t = qr_ref.shape[1]
    qb = QB_NSA
    pair = pl.program_id(2)
    _fill_value_slab(vsa_ref, vs_ref[0])
    _fill_value_slab(vwa_ref, vw_ref[0])
    lane_t = lax.broadcasted_iota(jnp.int32, (t, LANES), 1)
    key_blk = lax.shift_right_logical(lax.broadcasted_iota(jnp.int32, (t, LANES), 0), SLC_SHIFT)
    onehot = jnp.where((lane_t & (HEAD_DIM - 1)) == key_blk, 1.0, 0.0).astype(BF16)
    kse_ref[...] = jnp.where(lane_t < HEAD_DIM, ks_ref[0], onehot)

    lane = lax.broadcasted_iota(jnp.int32, (qb, LANES), 1)
    lo_lane = lane < HEAD_DIM
    diag_bias = _causal_bias(qb)

    def win_bias(w0, r0):
        n = r0 + qb - w0
        kpos = w0 + lax.broadcasted_iota(jnp.int32, (qb, n), 1)
        qpos = r0 + lax.broadcasted_iota(jnp.int32, (qb, n), 0)
        return jnp.where(kpos <= qpos, jnp.where(kpos > qpos - WIN, 0.0, NEG_INF), NEG_INF)

    steady_win_bias = win_bias(0, WIN)

    def normalised(oa):
        return oa[:, :LANES] / oa[:, LANES:]

    for i in reversed(range(t // qb)):
        r0 = i * qb
        rows = slice(r0, r0 + qb)
        qr = qr_ref[0, rows, :]
        zero = jnp.zeros_like(qr)
        selb = sel_ref[0, 0, rows, :]
        q_hi = pltpu.roll(qr.astype(F32), HEAD_DIM, 1).astype(BF16)
        qs = jnp.concatenate([jnp.where(lo_lane, qr, selb), jnp.where(lo_lane, q_hi, selb)], axis=0)
        p = _causal_probs(_dot_nt(qs, kse_ref[:r0 + qb, :]).reshape(2, qb, r0 + qb), diag_bias)
        o_slc = normalised(_dot(p.reshape(2 * qb, r0 + qb), vsa_ref[:r0 + qb, :]))

        w0 = max(r0 - WIN, 0)
        bias_w = steady_win_bias if r0 >= WIN else win_bias(w0, r0)
        q2 = jnp.concatenate([jnp.where(lo_lane, qr, zero), jnp.where(lo_lane, zero, qr)], axis=0)
        sw = _dot_nt(q2, kw_ref[0, w0:r0 + qb, :]).reshape(2, qb, r0 + qb - w0) + bias_w[None]
        pw = jnp.exp2(sw - jnp.max(sw, axis=2, keepdims=True)).astype(BF16).reshape(2 * qb, r0 + qb - w0)
        o_win = normalised(_dot(pw, vwa_ref[w0:r0 + qb, :]))

        gate = gate_ref[0, rows, :]
        o_cmp = ocmp_ref[0, rows, :]
        outs = []
        for par in range(2):
            col = 6 * pair + 3 * par
            g = [jnp.sum(jnp.where(lane == col + br, gate, 0.0), axis=1, keepdims=True) for br in range(3)]
            hr = slice(par * qb, (par + 1) * qb)
            outs.append(g[0] * o_cmp + g[1] * o_slc[hr] + g[2] * o_win[hr])
        o_ref[0, rows, :] = jnp.where(lo_lane, outs[0], outs[1]).astype(o_ref.dtype)


def _nsa(qr, ocmp, ks2, vs2, kw2, vw2, gns, selb):
    b, t, _ = qr.shape
    pair = pl.BlockSpec((1, t, LANES), lambda i, g, p: (i, 0, g * PAIRS + p))
    grp = pl.BlockSpec((1, t, LANES), lambda i, g, p: (i, 0, g))
    sel = pl.BlockSpec((1, 1, t, LANES), lambda i, g, p: (i, g, 0, 0))
    slab = pltpu.VMEM((t, 2 * LANES), BF16)
    keys = pltpu.VMEM((t, LANES), BF16)
    return pl.pallas_call(
        _nsa_kernel,
        grid=(b, NSA_GROUPS, PAIRS),
        in_specs=[pair, pair, grp, grp, grp, grp, grp, sel],
        out_specs=pair,
        out_shape=jax.ShapeDtypeStruct((b, t, NSA_W), BF16),
        scratch_shapes=[slab, slab, keys],
        compiler_params=_params("parallel", "parallel", "parallel"),
        name="nsa_attn",
    )(qr, ocmp, ks2, vs2, kw2, vw2, gns, selb)


def _merge_ffn_kernel(x_ref, yda_ref, yns_ref, ga_ref, gb_ref, wa_ref, wb_ref, wo_ref,
                      g_ref, w1_ref, w3_ref, w2_ref, fg_ref, o_ref):
    merged = ga_ref[...] * _dot(yda_ref[...], wa_ref[...]) + gb_ref[...] * _dot(yns_ref[...], wb_ref[...])
    x = x_ref[...] + _dot(merged.astype(BF16), wo_ref[...])
    o_ref[...] = _rms(_swiglu_residual(x, g_ref[...], w1_ref, w3_ref, w2_ref), fg_ref[...])


def _merge_ffn(x2, yda, yns, ga, gb, w_da, w_nsa, w_out, g, w1, w3, w2, final_g):
    n = x2.shape[0]
    row = pl.BlockSpec((TM_FFN, D_MODEL), lambda i: (i, 0))
    wspec = _resident((D_MODEL, D_MODEL))
    return pl.pallas_call(
        _merge_ffn_kernel,
        grid=(n // TM_FFN,),
        in_specs=[row] * 5 + [wspec] * 3 + _ffn_specs() + [_resident((1, D_MODEL))],
        out_specs=row,
        out_shape=jax.ShapeDtypeStruct((n, D_MODEL), F32),
        compiler_params=_params("parallel"),
        name="merge_ffn",
    )(x2, yda, yns, ga, gb, w_da.astype(BF16), w_nsa.astype(BF16), w_out.astype(BF16),
      g.reshape(1, D_MODEL), w1.astype(BF16), w3.astype(BF16), w2.astype(BF16), final_g.reshape(1, D_MODEL))


def kernel(x, ffn1_norm, ffn1_w1, ffn1_w3, ffn1_w2, mix_norm, w_in, da_lambda, da_head_norm, cmp_pos, cmp_w1,
           cmp_w2, w_proj_da, w_proj_nsa, w_out, ffn2_norm, ffn2_w1, ffn2_w3, ffn2_w2, final_norm):
    b, t, d = x.shape
    depth = ffn1_norm.shape[0]
    assert depth == 1, "one decoder layer per call"
    lam_init = 0.8 - 0.6 * math.exp(-0.3 * 0)
    x2 = _ffn(x.reshape(b * t, d), ffn1_norm[0], ffn1_w1[0], ffn1_w3[0], ffn1_w2[0])
    (qda, kda, vda, qnr, qnc, kcv, ks2, vs2, kw2, vw2, gns, ga, gb) = _inproj(
        x2.reshape(b, t, d), mix_norm[0], w_in[0])
    yda = _diff_attention(qda, kda, vda, da_lambda[0], da_head_norm[0], lam_init)
    kvc = _compress(kcv, cmp_pos[0], cmp_w1[0], cmp_w2[0])
    selb, ocmp = _select(qnc, kvc)
    yns = _nsa(qnr, ocmp, ks2, vs2, kw2, vw2, gns, selb)
    flat = lambda a: a.reshape(b * t, a.shape[-1])
    out = _merge_ffn(x2, flat(yda), flat(yns), flat(ga), flat(gb), w_proj_da[0], w_proj_nsa[0], w_out[0],
                     ffn2_norm[0], ffn2_w1[0], ffn2_w3[0], ffn2_w2[0], final_norm)
    return out.reshape(b, t, d)
```

```python
import functools
import math

import jax
import jax.numpy as jnp
from jax import lax
from jax.experimental import pallas as pl
from jax.experimental.pallas import tpu as pltpu

D_MODEL = 1024
HEAD_DIM = 64
DA_HEADS = 8
NSA_HEADS = 16
NSA_GROUPS = 2
NSA_HPG = NSA_HEADS // NSA_GROUPS
CMP_BLOCK = 32
CMP_STRIDE = 16
CMP_HIDDEN = 256
SLC_BLOCK = 64
SLC_TOPN = 16
WIN = 512
D_FF = 2816
ROPE_THETA = 10000.0
EPS = 1e-6
NEG_INF = -1e30
FORCE_SCORE = 1e9
LOG2E = math.log2(math.e)

LANES = 128
DA_W = DA_HEADS * 2 * HEAD_DIM
NSA_W = NSA_HEADS * HEAD_DIM
GROUP_W = NSA_HPG * HEAD_DIM
PAIRS = NSA_HPG // 2
KV_W = NSA_GROUPS * HEAD_DIM
GATE_W = 3 * NSA_HEADS
GATE_PER_GROUP = 3 * NSA_HPG
SLC_SHIFT = SLC_BLOCK.bit_length() - 1
assert 1 << SLC_SHIFT == SLC_BLOCK

TM_FFN = 512
TM_PROJ = 256
FF_CHUNK = 256
QB_DA = 256
QB_NSA = 256
assert QB_NSA & (QB_NSA - 1) == 0
VMEM_LIMIT = 56 * 1024 * 1024

BF16 = jnp.bfloat16
F32 = jnp.float32


def _rms(x, g):
    return x * lax.rsqrt(jnp.mean(x * x, axis=-1, keepdims=True) + EPS) * g


def _dot(a, b):
    return jnp.dot(a, b, preferred_element_type=F32)


def _dot_nt(a, b):
    return lax.dot_general(a, b, (((1,), (1,)), ((), ())), preferred_element_type=F32)


def _resident(shape):
    nd = len(shape)
    return pl.BlockSpec(shape, lambda *_: (0,) * nd, pipeline_mode=pl.Buffered(1))


def _params(*sem):
    return pltpu.CompilerParams(dimension_semantics=sem, vmem_limit_bytes=VMEM_LIMIT)


def _swiglu_residual(x, g, w1_ref, w3_ref, w2_ref):
    h = _rms(x, g).astype(BF16)
    acc = jnp.zeros_like(x)
    for c in range(D_FF // FF_CHUNK):
        sl = slice(c * FF_CHUNK, (c + 1) * FF_CHUNK)
        a = _dot(h, w1_ref[:, sl])
        b = _dot(h, w3_ref[:, sl])
        u = (a * jax.nn.sigmoid(a) * b).astype(BF16)
        acc = acc + _dot(u, w2_ref[sl, :])
    return x + 0.5 * acc


def _ffn_kernel(x_ref, g_ref, w1_ref, w3_ref, w2_ref, o_ref):
    o_ref[...] = _swiglu_residual(x_ref[...], g_ref[...], w1_ref, w3_ref, w2_ref)


def _ffn_specs():
    return [_resident((1, D_MODEL)), _resident((D_MODEL, D_FF)), _resident((D_MODEL, D_FF)),
            _resident((D_FF, D_MODEL))]


def _ffn(x2, g, w1, w3, w2):
    n = x2.shape[0]
    row = pl.BlockSpec((TM_FFN, D_MODEL), lambda i: (i, 0))
    return pl.pallas_call(
        _ffn_kernel,
        grid=(n // TM_FFN,),
        in_specs=[row] + _ffn_specs(),
        out_specs=row,
        out_shape=jax.ShapeDtypeStruct((n, D_MODEL), F32),
        compiler_params=_params("parallel"),
        name="ffn",
    )(x2, g.reshape(1, D_MODEL), w1.astype(BF16), w3.astype(BF16), w2.astype(BF16))


_C_QDA, _C_KDA, _C_VDA, _C_QNS = 0, 1024, 2048, 3072
_C_KCV = 4096
_C_KS2 = 4352
_C_VS2 = 4608
_C_KW2 = 4864
_C_VW2 = 5120
_C_GNS = 5376
_C_GA = 5632
_C_GB = 6656
_MAIN_W = 7680


def _proj_kernel(x_ref, g_ref, w_ref, cos_ref, sa_ref, sb_ref,
                 qda_ref, kda_ref, vda_ref, qnr_ref, qnc_ref, kcr_ref, vcr_ref,
                 ks2_ref, vs2_ref, kw2_ref, vw2_ref, gns_ref, ga_ref, gb_ref):
    h = _rms(x_ref[0], g_ref[...]).astype(BF16)
    cos, sa, sb = cos_ref[...], sa_ref[...], sb_ref[...]

    def proj(c0, width):
        return _dot(h, w_ref[:, c0:c0 + width])

    def rope(y):
        cols = []
        for c in range(y.shape[1] // LANES):
            v = y[:, c * LANES:(c + 1) * LANES]
            cols.append(v * cos + pltpu.roll(v, LANES - 32, 1) * sa + pltpu.roll(v, 32, 1) * sb)
        return cols[0] if len(cols) == 1 else jnp.concatenate(cols, axis=1)

    qscale = HEAD_DIM ** -0.5 * LOG2E
    half = DA_W // 2
    for c0 in (0, half):
        qda_ref[0, :, c0:c0 + half] = (rope(proj(_C_QDA + c0, half)) * qscale).astype(BF16)
        kda_ref[0, :, c0:c0 + half] = rope(proj(_C_KDA + c0, half)).astype(BF16)
        vda_ref[0, :, c0:c0 + half] = proj(_C_VDA + c0, half).astype(BF16)
        qn = proj(_C_QNS + c0, half)
        qnc_ref[0, :, c0:c0 + half] = (qn * qscale).astype(BF16)
        qnr_ref[0, :, c0:c0 + half] = (rope(qn) * qscale).astype(BF16)
        ga_ref[0, :, c0:c0 + half] = jax.nn.sigmoid(proj(_C_GA + c0, half))
        gb_ref[0, :, c0:c0 + half] = jax.nn.sigmoid(proj(_C_GB + c0, half))
    kcv = proj(_C_KCV, 256)
    kcr_ref[0] = kcv[:, :KV_W]
    vcr_ref[0] = kcv[:, KV_W:]
    ks2_ref[0] = rope(proj(_C_KS2, 256)).astype(BF16)
    vs2_ref[0] = proj(_C_VS2, 256).astype(BF16)
    kw2_ref[0] = rope(proj(_C_KW2, 256)).astype(BF16)
    vw2_ref[0] = proj(_C_VW2, 256).astype(BF16)
    gns_ref[0] = jax.nn.sigmoid(proj(_C_GNS, 256))


def _proj_weights(w_in):
    offs = [0]
    for w in (DA_W, DA_W, DA_W, NSA_W) + (KV_W,) * 6 + (GATE_W, D_MODEL, D_MODEL):
        offs.append(offs[-1] + w)
    (o_q, o_k, o_v, o_qn, o_kc, o_vc, o_ks, o_vs, o_kw, o_vw, o_g, o_ga, o_gb, _) = offs

    def dup(o):
        parts = []
        for g in range(NSA_GROUPS):
            blk = w_in[:, o + g * HEAD_DIM:o + (g + 1) * HEAD_DIM]
            parts += [blk, blk]
        return jnp.concatenate(parts, axis=1)

    gate_parts = []
    for g in range(NSA_GROUPS):
        gate_parts.append(jnp.pad(w_in[:, o_g + g * GATE_PER_GROUP:o_g + (g + 1) * GATE_PER_GROUP],
                                  ((0, 0), (0, LANES - GATE_PER_GROUP))))
    slab = jnp.concatenate(
        [w_in[:, o_q:o_kc], w_in[:, o_kc:o_ks], dup(o_ks), dup(o_vs), dup(o_kw), dup(o_vw)]
        + gate_parts + [w_in[:, o_ga:o_gb], w_in[:, o_gb:]], axis=1)
    return slab.astype(BF16)


def _rope_tables(t):
    pos = jnp.arange(t, dtype=F32)
    inv = 1.0 / (ROPE_THETA ** (jnp.arange(0, HEAD_DIM, 2, dtype=F32) / HEAD_DIM))
    ang = pos[:, None] * inv[None, :]
    cos, sin = jnp.cos(ang), jnp.sin(ang)
    zero = jnp.zeros_like(sin)
    cos128 = jnp.tile(cos, (1, 4))
    sa = jnp.tile(jnp.concatenate([-sin, zero], axis=1), (1, 2))
    sb = jnp.tile(jnp.concatenate([zero, sin], axis=1), (1, 2))
    return cos128, sa, sb


def _inproj(x3, g, w_in):
    b, t, _ = x3.shape
    cos, sa, sb = _rope_tables(t)
    tok = lambda w: pl.BlockSpec((1, TM_PROJ, w), lambda i, j: (i, j, 0))
    tab = pl.BlockSpec((TM_PROJ, LANES), lambda i, j: (j, 0))
    sds = lambda w, dt: jax.ShapeDtypeStruct((b, t, w), dt)
    outs = [(DA_W, BF16)] * 3 + [(NSA_W, BF16)] * 2 + [(KV_W, F32)] * 2 + [(256, BF16)] * 4 \
        + [(256, F32), (D_MODEL, F32), (D_MODEL, F32)]
    return pl.pallas_call(
        _proj_kernel,
        grid=(b, t // TM_PROJ),
        in_specs=[tok(D_MODEL), _resident((1, D_MODEL)), _resident((D_MODEL, _MAIN_W)), tab, tab, tab],
        out_specs=[tok(w) for w, _ in outs],
        out_shape=[sds(w, dt) for w, dt in outs],
        compiler_params=_params("parallel", "parallel"),
        name="inproj",
    )(x3, g.reshape(1, D_MODEL), _proj_weights(w_in), cos, sa, sb)


def _fill_value_slab(va_ref, v):
    va_ref[:, :LANES] = v
    va_ref[:, LANES:] = jnp.ones(v.shape, v.dtype)


def _causal_bias(qb):
    tri = lax.broadcasted_iota(jnp.int32, (qb, qb), 1) <= lax.broadcasted_iota(jnp.int32, (qb, qb), 0)
    return jnp.where(tri, 0.0, NEG_INF)


def _causal_probs(s, diag_bias):
    qb = s.shape[1]
    r0 = s.shape[2] - qb
    sd = s[:, :, r0:] + diag_bias[None]
    m = jnp.max(sd, axis=2, keepdims=True)
    if not r0:
        return jnp.exp2(sd - m).astype(BF16)
    sm = s[:, :, :r0]
    m = jnp.maximum(m, jnp.max(sm, axis=2, keepdims=True))
    return jnp.concatenate([jnp.exp2(sm - m).astype(BF16), jnp.exp2(sd - m).astype(BF16)], axis=2)


def _da_kernel(lam_ref, gain_ref, q_ref, k_ref, v_ref, o_ref, va_ref, *, lam_init):
    t = q_ref.shape[1]
    qb = QB_DA
    _fill_value_slab(va_ref, v_ref[0])
    lp = lam_ref[...]
    lam = (jnp.exp(jnp.sum(lp[0:1] * lp[1:2], axis=1, keepdims=True))
           - jnp.exp(jnp.sum(lp[2:3] * lp[3:4], axis=1, keepdims=True)) + lam_init)
    gain = gain_ref[0] * (1.0 - lam_init)
    lane = lax.broadcasted_iota(jnp.int32, (qb, LANES), 1)
    diag_bias = _causal_bias(qb)

    for i in reversed(range(t // qb)):
        r0 = i * qb
        q = q_ref[0, r0:r0 + qb, :]
        zero = jnp.zeros_like(q)
        qs = jnp.concatenate([jnp.where(lane < HEAD_DIM, q, zero), jnp.where(lane >= HEAD_DIM, q, zero)], axis=0)
        p = _causal_probs(_dot_nt(qs, k_ref[0, :r0 + qb, :]).reshape(2, qb, r0 + qb), diag_bias)
        oa = _dot(p.reshape(2 * qb, r0 + qb), va_ref[:r0 + qb, :])
        o = oa[:, :LANES] / oa[:, LANES:]
        o = o[:qb] - lam * o[qb:]
        o_ref[0, r0:r0 + qb, :] = (_rms(o, gain)).astype(o_ref.dtype)


def _diff_attention(q, k, v, da_lambda, head_gain, lam_init):
    b, t, _ = q.shape
    head = pl.BlockSpec((1, t, LANES), lambda i, h: (i, 0, h))
    return pl.pallas_call(
        functools.partial(_da_kernel, lam_init=lam_init),
        grid=(b, DA_HEADS),
        in_specs=[pl.BlockSpec((4, HEAD_DIM), lambda i, h: (0, 0)),
                  pl.BlockSpec((1, 1, LANES), lambda i, h: (h, 0, 0)), head, head, head],
        out_specs=head,
        out_shape=jax.ShapeDtypeStruct((b, t, DA_W), BF16),
        scratch_shapes=[pltpu.VMEM((t, 2 * LANES), BF16)],
        compiler_params=_params("parallel", "parallel"),
        name="diffattn",
    )(da_lambda, head_gain.reshape(DA_HEADS, 1, LANES), q, k, v)


def _cmp_kernel(k_ref, v_ref, pos_ref, w1_ref, w2_ref, o_ref):
    assert NSA_GROUPS * HEAD_DIM == LANES
    half = CMP_STRIDE * HEAD_DIM
    n = k_ref.shape[1] // CMP_STRIDE
    lo_lane = lax.broadcasted_iota(jnp.int32, (n, LANES), 1) < HEAD_DIM
    for kv, x_ref in enumerate((k_ref, v_ref)):
        tok = [x_ref[0, pl.ds(l, n, stride=CMP_STRIDE), :] for l in range(CMP_STRIDE)]
        chunks = [[], []]
        for l in range(0, CMP_STRIDE, 2):
            a, b = tok[l], tok[l + 1]
            chunks[0].append(jnp.where(lo_lane, a, pltpu.roll(b, HEAD_DIM, 1)))
            chunks[1].append(jnp.where(lo_lane, pltpu.roll(a, HEAD_DIM, 1), b))
        for g in range(NSA_GROUPS):
            x = jnp.concatenate(chunks[g], axis=1)
            pos = pos_ref[kv]
            xa = (x + pos[:, :half]).astype(BF16)
            xb = (x + pos[:, half:]).astype(BF16)
            a = _dot(xa, w1_ref[kv, :half, :])
            bm = _dot(xb, w1_ref[kv, half:, :])
            hid = a + pltpu.roll(bm, bm.shape[0] - 1, 0)
            act = jax.nn.gelu(hid).astype(BF16)
            o_ref[0, kv * NSA_GROUPS + g] = _dot(act, w2_ref[kv]).astype(o_ref.dtype)


def _compress(kcr, vcr, cmp_pos, cmp_w1, cmp_w2):
    b, t, _ = kcr.shape
    n = t // CMP_STRIDE
    w2d = jnp.concatenate([cmp_w2, cmp_w2], axis=-1).astype(BF16)
    raw = pl.BlockSpec((1, t, KV_W), lambda i: (i, 0, 0))
    return pl.pallas_call(
        _cmp_kernel,
        grid=(b,),
        in_specs=[raw, raw,
                  pl.BlockSpec((2, 1, CMP_BLOCK * HEAD_DIM), lambda i: (0, 0, 0)),
                  pl.BlockSpec((2, CMP_BLOCK * HEAD_DIM, CMP_HIDDEN), lambda i: (0, 0, 0)),
                  pl.BlockSpec((2, CMP_HIDDEN, LANES), lambda i: (0, 0, 0))],
        out_specs=pl.BlockSpec((1, 2 * NSA_GROUPS, n, LANES), lambda i: (i, 0, 0, 0)),
        out_shape=jax.ShapeDtypeStruct((b, 2 * NSA_GROUPS, n, LANES), BF16),
        compiler_params=_params("parallel"),
        name="compress",
    )(kcr, vcr, cmp_pos.reshape(2, 1, CMP_BLOCK * HEAD_DIM), cmp_w1.astype(BF16), w2d)


def _split3(x):
    hi = x.astype(BF16)
    r1 = x - hi.astype(F32)
    mid = r1.astype(BF16)
    lo = (r1 - mid.astype(F32)).astype(BF16)
    return hi, mid, lo


def _stack_heads(qblk, heads, lo_lane):
    zero = jnp.zeros((qblk.shape[0], LANES), qblk.dtype)
    parts = []
    for j in heads:
        pair = qblk[:, (j // 2) * LANES:(j // 2 + 1) * LANES]
        parts.append(jnp.where(lo_lane if j % 2 == 0 else ~lo_lane, pair, zero))
    return jnp.concatenate(parts, axis=0)


def _sel_kernel(qc_ref, kc_ref, vc_ref, sel_ref, ocmp_ref):
    t = qc_ref.shape[1]
    qb, hp = QB_NSA, NSA_HPG
    n_slc = t // SLC_BLOCK
    lane = lax.broadcasted_iota(jnp.int32, (qb, LANES), 1)
    lo_lane = lane < HEAD_DIM
    bias_lane = (lane & (HEAD_DIM - 1)) < n_slc
    ov_s = lax.broadcasted_iota(jnp.int32, (LANES, LANES), 0)
    ov_n = lax.broadcasted_iota(jnp.int32, (LANES, LANES), 1)
    ov_t = jnp.where(ov_s < n_slc,
                     jnp.where(ov_n * CMP_STRIDE < (ov_s + 1) * SLC_BLOCK,
                               jnp.where(ov_n * CMP_STRIDE + CMP_BLOCK > ov_s * SLC_BLOCK, 1.0, 0.0), 0.0),
                     0.0).astype(BF16)
    s_idx = lax.broadcasted_iota(jnp.int32, (n_slc, qb), 0)
    t_off = lax.broadcasted_iota(jnp.int32, (n_slc, qb), 1)
    pad = jnp.zeros((HEAD_DIM - n_slc, qb), F32)
    vc_t = vc_ref[0, 0].astype(F32).T[:HEAD_DIM].astype(BF16)

    for i in range(t // qb):
        r0 = i * qb
        rows = slice(r0, r0 + qb)
        n_any = min(LANES, (r0 + qb) // CMP_STRIDE)
        n_all = (max(r0 - (CMP_BLOCK - 1) + CMP_STRIDE, 0) // CMP_STRIDE) // 8 * 8
        q_c = _stack_heads(qc_ref[0, rows, :], range(hp), lo_lane)
        s = _dot_nt(kc_ref[0, 0, :n_any, :], q_c)
        blk = n_all + lax.broadcasted_iota(jnp.int32, (n_any - n_all, hp * qb), 0)
        tq = r0 + (lax.broadcasted_iota(jnp.int32, (n_any - n_all, hp * qb), 1) & (qb - 1))
        valid = (blk * CMP_STRIDE + (CMP_BLOCK - 1)) <= tq
        edge = s[n_all:] + jnp.where(valid, 0.0, NEG_INF)
        m = jnp.max(edge, axis=0, keepdims=True)
        if n_all:
            m = jnp.maximum(m, jnp.max(s[:n_all], axis=0, keepdims=True))
        e_edge = jnp.exp2(edge - m)
        l = jnp.sum(e_edge, axis=0, keepdims=True)
        if n_all:
            e_all = jnp.exp2(s[:n_all] - m)
            l = l + jnp.sum(e_all, axis=0, keepdims=True)
        inv = 1.0 / l
        p = jnp.where(valid, e_edge * inv, 0.0)
        if n_all:
            p = jnp.concatenate([e_all * inv, p], axis=0)
        pb = p.astype(BF16)
        if n_any < LANES:
            pb = jnp.concatenate([pb, jnp.zeros((LANES - n_any, hp * qb), BF16)], axis=0)
        o_t = _dot(vc_t, pb)
        for pr in range(hp // 2):
            both = jnp.concatenate([o_t[:, (2 * pr) * qb:(2 * pr + 1) * qb],
                                    o_t[:, (2 * pr + 1) * qb:(2 * pr + 2) * qb]], axis=0)
            ocmp_ref[0, rows, pr * LANES:(pr + 1) * LANES] = both.T
        psum = sum(p[:, j * qb:(j + 1) * qb] for j in range(hp))
        if n_any < LANES:
            psum = jnp.concatenate([psum, jnp.zeros((LANES - n_any, qb), F32)], axis=0)
        imp_t = sum(_dot(ov_t, term) for term in _split3(psum))
        blk_t = lax.shift_right_logical(r0 + t_off, SLC_SHIFT)
        causal = s_idx <= blk_t
        n_causal = (r0 + qb) // SLC_BLOCK
        if n_causal <= SLC_TOPN:
            sel_t = jnp.where(causal, 1.0, 0.0)
        else:
            forced = (s_idx == 0) | (s_idx == blk_t) | (s_idx == blk_t - 1)
            score = jnp.where(forced, FORCE_SCORE, jnp.where(causal, imp_t[:n_slc], -1.0))
            rank = jnp.zeros((n_slc, qb), F32)
            for sp in range(n_causal):
                other = score[sp:sp + 1, :]
                tie = jnp.where(s_idx > sp, 1.0, 0.0)
                rank = rank + jnp.where(other > score, 1.0, jnp.where(other == score, tie, 0.0))
            sel_t = jnp.where(causal, jnp.where(rank < SLC_TOPN, 1.0, 0.0), 0.0)
        both = jnp.concatenate([sel_t, pad, sel_t, pad], axis=0).T
        sel_ref[0, 0, rows, :] = jnp.where(bias_lane, jnp.where(both > 0.5, 0.0, NEG_INF), 0.0).astype(BF16)


def _select(qc, kvc):
    b, t, _ = qc.shape
    n = t // CMP_STRIDE
    grp = pl.BlockSpec((1, t, GROUP_W), lambda i, g: (i, 0, g))
    return pl.pallas_call(
        _sel_kernel,
        grid=(b, NSA_GROUPS),
        in_specs=[grp, pl.BlockSpec((1, 1, n, LANES), lambda i, g: (i, g, 0, 0)),
                  pl.BlockSpec((1, 1, n, LANES), lambda i, g: (i, NSA_GROUPS + g, 0, 0))],
        out_specs=[pl.BlockSpec((1, 1, t, LANES), lambda i, g: (i, g, 0, 0)), grp],
        out_shape=[jax.ShapeDtypeStruct((b, NSA_GROUPS, t, LANES), BF16),
                   jax.ShapeDtypeStruct((b, t, NSA_W), F32)],
        compiler_params=_params("parallel", "parallel"),
        name="nsa_sel",
    )(qc, kvc, kvc)


def _nsa_kernel(qr_ref, ocmp_ref, ks_ref, vs_ref, kw_ref, vw_ref, gate_ref, sel_ref, o_ref,
                vsa_ref, vwa_ref, kse_ref):
    t = qr_ref.shape[1]
    qb = QB_NSA
    pair = pl.program_id(2)
    _fill_value_slab(vsa_ref, vs_ref[0])
    _fill_value_slab(vwa_ref, vw_ref[0])
    lane_t = lax.broadcasted_iota(jnp.int32, (t, LANES), 1)
    key_blk = lax.shift_right_logical(lax.broadcasted_iota(jnp.int32, (t, LANES), 0), SLC_SHIFT)
    onehot = jnp.where((lane_t & (HEAD_DIM - 1)) == key_blk, 1.0, 0.0).astype(BF16)
    kse_ref[...] = jnp.where(lane_t < HEAD_DIM, ks_ref[0], onehot)

    lane = lax.broadcasted_iota(jnp.int32, (qb, LANES), 1)
    lo_lane = lane < HEAD_DIM
    diag_bias = _causal_bias(qb)

    def win_bias(w0, r0):
        n = r0 + qb - w0
        kpos = w0 + lax.broadcasted_iota(jnp.int32, (qb, n), 1)
        qpos = r0 + lax.broadcasted_iota(jnp.int32, (qb, n), 0)
        return jnp.where(kpos <= qpos, jnp.where(kpos > qpos - WIN, 0.0, NEG_INF), NEG_INF)

    steady_win_bias = win_bias(0, WIN)

    def normalised(oa):
        return oa[:, :LANES] / oa[:, LANES:]

    for i in reversed(range(t // qb)):
        r0 = i * qb
        rows = slice(r0, r0 + qb)
        qr = qr_ref[0, rows, :]
        zero = jnp.zeros_like(qr)
        selb = sel_ref[0, 0, rows, :]
        q_hi = pltpu.roll(qr.astype(F32), HEAD_DIM, 1).astype(BF16)
        qs = jnp.concatenate([jnp.where(lo_lane, qr, selb), jnp.where(lo_lane, q_hi, selb)], axis=0)
        p = _causal_probs(_dot_nt(qs, kse_ref[:r0 + qb, :]).reshape(2, qb, r0 + qb), diag_bias)
        o_slc = normalised(_dot(p.reshape(2 * qb, r0 + qb), vsa_ref[:r0 + qb, :]))

        w0 = max(r0 - WIN, 0)
        bias_w = steady_win_bias if r0 >= WIN else win_bias(w0, r0)
        q2 = jnp.concatenate([jnp.where(lo_lane, qr, zero), jnp.where(lo_lane, zero, qr)], axis=0)
        sw = _dot_nt(q2, kw_ref[0, w0:r0 + qb, :]).reshape(2, qb, r0 + qb - w0) + bias_w[None]
        pw = jnp.exp2(sw - jnp.max(sw, axis=2, keepdims=True)).astype(BF16).reshape(2 * qb, r0 + qb - w0)
        o_win = normalised(_dot(pw, vwa_ref[w0:r0 + qb, :]))

        gate = gate_ref[0, rows, :]
        o_cmp = ocmp_ref[0, rows, :]
        outs = []
        for par in range(2):
            col = 6 * pair + 3 * par
            g = [jnp.sum(jnp.where(lane == col + br, gate, 0.0), axis=1, keepdims=True) for br in range(3)]
            hr = slice(par * qb, (par + 1) * qb)
            outs.append(g[0] * o_cmp + g[1] * o_slc[hr] + g[2] * o_win[hr])
        o_ref[0, rows, :] = jnp.where(lo_lane, outs[0], outs[1]).astype(o_ref.dtype)


def _nsa(qr, ocmp, ks2, vs2, kw2, vw2, gns, selb):
    b, t, _ = qr.shape
    pair = pl.BlockSpec((1, t, LANES), lambda i, g, p: (i, 0, g * PAIRS + p))
    grp = pl.BlockSpec((1, t, LANES), lambda i, g, p: (i, 0, g))
    sel = pl.BlockSpec((1, 1, t, LANES), lambda i, g, p: (i, g, 0, 0))
    slab = pltpu.VMEM((t, 2 * LANES), BF16)
    keys = pltpu.VMEM((t, LANES), BF16)
    return pl.pallas_call(
        _nsa_kernel,
        grid=(b, NSA_GROUPS, PAIRS),
        in_specs=[pair, pair, grp, grp, grp, grp, grp, sel],
        out_specs=pair,
        out_shape=jax.ShapeDtypeStruct((b, t, NSA_W), BF16),
        scratch_shapes=[slab, slab, keys],
        compiler_params=_params("parallel", "parallel", "parallel"),
        name="nsa_attn",
    )(qr, ocmp, ks2, vs2, kw2, vw2, gns, selb)


def _merge_ffn_kernel(x_ref, yda_ref, yns_ref, ga_ref, gb_ref, wa_ref, wb_ref, wo_ref,
                      g_ref, w1_ref, w3_ref, w2_ref, fg_ref, o_ref):
    merged = ga_ref[...] * _dot(yda_ref[...], wa_ref[...]) + gb_ref[...] * _dot(yns_ref[...], wb_ref[...])
    x = x_ref[...] + _dot(merged.astype(BF16), wo_ref[...])
    o_ref[...] = _rms(_swiglu_residual(x, g_ref[...], w1_ref, w3_ref, w2_ref), fg_ref[...])


def _merge_ffn(x2, yda, yns, ga, gb, w_da, w_nsa, w_out, g, w1, w3, w2, final_g):
    n = x2.shape[0]
    row = pl.BlockSpec((TM_FFN, D_MODEL), lambda i: (i, 0))
    wspec = _resident((D_MODEL, D_MODEL))
    return pl.pallas_call(
        _merge_ffn_kernel,
        grid=(n // TM_FFN,),
        in_specs=[row] * 5 + [wspec] * 3 + _ffn_specs() + [_resident((1, D_MODEL))],
        out_specs=row,
        out_shape=jax.ShapeDtypeStruct((n, D_MODEL), F32),
        compiler_params=_params("parallel"),
        name="merge_ffn",
    )(x2, yda, yns, ga, gb, w_da.astype(BF16), w_nsa.astype(BF16), w_out.astype(BF16),
      g.reshape(1, D_MODEL), w1.astype(BF16), w3.astype(BF16), w2.astype(BF16), final_g.reshape(1, D_MODEL))


def kernel(x, ffn1_norm, ffn1_w1, ffn1_w3, ffn1_w2, mix_norm, w_in, da_lambda, da_head_norm, cmp_pos, cmp_w1,
           cmp_w2, w_proj_da, w_proj_nsa, w_out, ffn2_norm, ffn2_w1, ffn2_w3, ffn2_w2, final_norm):
    b, t, d = x.shape
    depth = ffn1_norm.shape[0]
    assert depth == 1, "one decoder layer per call"
    lam_init = 0.8 - 0.6 * math.exp(-0.3 * 0)
    x2 = _ffn(x.reshape(b * t, d), ffn1_norm[0], ffn1_w1[0], ffn1_w3[0], ffn1_w2[0])
    (qda, kda, vda, qnr, qnc, kcr, vcr, ks2, vs2, kw2, vw2, gns, ga, gb) = _inproj(
        x2.reshape(b, t, d), mix_norm[0], w_in[0])
    yda = _diff_attention(qda, kda, vda, da_lambda[0], da_head_norm[0], lam_init)
    kvc = _compress(kcr, vcr, cmp_pos[0], cmp_w1[0], cmp_w2[0])
    selb, ocmp = _select(qnc, kvc)
    yns = _nsa(qnr, ocmp, ks2, vs2, kw2, vw2, gns, selb)
    flat = lambda a: a.reshape(b * t, a.shape[-1])
    out = _merge_ffn(x2, flat(yda), flat(yns), flat(ga), flat(gb), w_proj_da[0], w_proj_nsa[0], w_out[0],
                     ffn2_norm[0], ffn2_w1[0], ffn2_w3[0], ffn2_w2[0], final_norm)
    return out.reshape(b, t, d)
```

```python
import functools
import math

import jax
import jax.numpy as jnp
from jax import lax
from jax.experimental import pallas as pl
from jax.experimental.pallas import tpu as pltpu

D_MODEL = 1024
HEAD_DIM = 64
DA_HEADS = 8
NSA_HEADS = 16
NSA_GROUPS = 2
NSA_HPG = NSA_HEADS // NSA_GROUPS
CMP_BLOCK = 32
CMP_STRIDE = 16
CMP_HIDDEN = 256
SLC_BLOCK = 64
SLC_TOPN = 16
WIN = 512
D_FF = 2816
ROPE_THETA = 10000.0
EPS = 1e-6
NEG_INF = -1e30
FORCE_SCORE = 1e9
LOG2E = math.log2(math.e)

LANES = 128
DA_W = DA_HEADS * 2 * HEAD_DIM
NSA_W = NSA_HEADS * HEAD_DIM
GROUP_W = NSA_HPG * HEAD_DIM
PAIRS = NSA_HPG // 2
KV_W = NSA_GROUPS * HEAD_DIM
GATE_W = 3 * NSA_HEADS
GATE_PER_GROUP = 3 * NSA_HPG
SLC_SHIFT = SLC_BLOCK.bit_length() - 1
assert 1 << SLC_SHIFT == SLC_BLOCK

TM_FFN = 512
TM_PROJ = 256
FF_CHUNK = 256
QB_DA = 256
QB_NSA = 256
PAIRS_PER_STEP = 1
LOOKAHEAD = 2
assert QB_NSA & (QB_NSA - 1) == 0
VMEM_LIMIT = 56 * 1024 * 1024

BF16 = jnp.bfloat16
F32 = jnp.float32


def _rms(x, g):
    return x * lax.rsqrt(jnp.mean(x * x, axis=-1, keepdims=True) + EPS) * g


def _dot(a, b):
    return jnp.dot(a, b, preferred_element_type=F32)


def _dot_nt(a, b):
    return lax.dot_general(a, b, (((1,), (1,)), ((), ())), preferred_element_type=F32)


def _resident(shape):
    nd = len(shape)
    return pl.BlockSpec(shape, lambda *_: (0,) * nd, pipeline_mode=pl.Buffered(1))


def _params(*sem):
    return pltpu.CompilerParams(dimension_semantics=sem, vmem_limit_bytes=VMEM_LIMIT)


def _swiglu_residual(x, g, w1_ref, w3_ref, w2_ref):
    h = _rms(x, g).astype(BF16)
    acc = jnp.zeros_like(x)
    for c in range(D_FF // FF_CHUNK):
        sl = slice(c * FF_CHUNK, (c + 1) * FF_CHUNK)
        a = _dot(h, w1_ref[:, sl])
        b = _dot(h, w3_ref[:, sl])
        u = (a * jax.nn.sigmoid(a) * b).astype(BF16)
        acc = acc + _dot(u, w2_ref[sl, :])
    return x + 0.5 * acc


def _ffn_kernel(x_ref, g_ref, w1_ref, w3_ref, w2_ref, o_ref):
    o_ref[...] = _swiglu_residual(x_ref[...], g_ref[...], w1_ref, w3_ref, w2_ref)


def _ffn_specs():
    return [_resident((1, D_MODEL)), _resident((D_MODEL, D_FF)), _resident((D_MODEL, D_FF)),
            _resident((D_FF, D_MODEL))]


def _ffn(x2, g, w1, w3, w2):
    n = x2.shape[0]
    row = pl.BlockSpec((TM_FFN, D_MODEL), lambda i: (i, 0))
    return pl.pallas_call(
        _ffn_kernel,
        grid=(n // TM_FFN,),
        in_specs=[row] + _ffn_specs(),
        out_specs=row,
        out_shape=jax.ShapeDtypeStruct((n, D_MODEL), F32),
        compiler_params=_params("parallel"),
        name="ffn",
    )(x2, g.reshape(1, D_MODEL), w1.astype(BF16), w3.astype(BF16), w2.astype(BF16))


_C_QDA, _C_KDA, _C_VDA, _C_QNS = 0, 1024, 2048, 3072
_C_KCV = 4096
_C_KS2 = 4352
_C_VS2 = 4608
_C_KW2 = 4864
_C_VW2 = 5120
_C_GNS = 5376
_C_GA = 5632
_C_GB = 6656
_MAIN_W = 7680


def _proj_kernel(x_ref, g_ref, w_ref, cos_ref, sa_ref, sb_ref,
                 qda_ref, kda_ref, vda_ref, qnr_ref, qnc_ref, kcr_ref, vcr_ref,
                 ks2_ref, vs2_ref, kw2_ref, vw2_ref, gns_ref, ga_ref, gb_ref):
    h = _rms(x_ref[0], g_ref[...]).astype(BF16)
    cos, sa, sb = cos_ref[...], sa_ref[...], sb_ref[...]

    def proj(c0, width):
        return _dot(h, w_ref[:, c0:c0 + width])

    def rope(y):
        cols = []
        for c in range(y.shape[1] // LANES):
            v = y[:, c * LANES:(c + 1) * LANES]
            cols.append(v * cos + pltpu.roll(v, LANES - 32, 1) * sa + pltpu.roll(v, 32, 1) * sb)
        return cols[0] if len(cols) == 1 else jnp.concatenate(cols, axis=1)

    qscale = HEAD_DIM ** -0.5 * LOG2E
    half = DA_W // 2
    for c0 in (0, half):
        qda_ref[0, :, c0:c0 + half] = (rope(proj(_C_QDA + c0, half)) * qscale).astype(BF16)
        kda_ref[0, :, c0:c0 + half] = rope(proj(_C_KDA + c0, half)).astype(BF16)
        vda_ref[0, :, c0:c0 + half] = proj(_C_VDA + c0, half).astype(BF16)
        qn = proj(_C_QNS + c0, half)
        qnc_ref[0, :, c0:c0 + half] = (qn * qscale).astype(BF16)
        qnr_ref[0, :, c0:c0 + half] = (rope(qn) * qscale).astype(BF16)
        ga_ref[0, :, c0:c0 + half] = jax.nn.sigmoid(proj(_C_GA + c0, half))
        gb_ref[0, :, c0:c0 + half] = jax.nn.sigmoid(proj(_C_GB + c0, half))
    kcv = proj(_C_KCV, 256)
    kcr_ref[0] = kcv[:, :KV_W]
    vcr_ref[0] = kcv[:, KV_W:]
    ks2_ref[0] = rope(proj(_C_KS2, 256)).astype(BF16)
    vs2_ref[0] = proj(_C_VS2, 256).astype(BF16)
    kw2_ref[0] = rope(proj(_C_KW2, 256)).astype(BF16)
    vw2_ref[0] = proj(_C_VW2, 256).astype(BF16)
    gns_ref[0] = jax.nn.sigmoid(proj(_C_GNS, 256))


def _proj_weights(w_in):
    offs = [0]
    for w in (DA_W, DA_W, DA_W, NSA_W) + (KV_W,) * 6 + (GATE_W, D_MODEL, D_MODEL):
        offs.append(offs[-1] + w)
    (o_q, o_k, o_v, o_qn, o_kc, o_vc, o_ks, o_vs, o_kw, o_vw, o_g, o_ga, o_gb, _) = offs

    def dup(o):
        parts = []
        for g in range(NSA_GROUPS):
            blk = w_in[:, o + g * HEAD_DIM:o + (g + 1) * HEAD_DIM]
            parts += [blk, blk]
        return jnp.concatenate(parts, axis=1)

    gate_parts = []
    for g in range(NSA_GROUPS):
        gate_parts.append(jnp.pad(w_in[:, o_g + g * GATE_PER_GROUP:o_g + (g + 1) * GATE_PER_GROUP],
                                  ((0, 0), (0, LANES - GATE_PER_GROUP))))
    slab = jnp.concatenate(
        [w_in[:, o_q:o_kc], w_in[:, o_kc:o_ks], dup(o_ks), dup(o_vs), dup(o_kw), dup(o_vw)]
        + gate_parts + [w_in[:, o_ga:o_gb], w_in[:, o_gb:]], axis=1)
    return slab.astype(BF16)


def _rope_tables(t):
    pos = jnp.arange(t, dtype=F32)
    inv = 1.0 / (ROPE_THETA ** (jnp.arange(0, HEAD_DIM, 2, dtype=F32) / HEAD_DIM))
    ang = pos[:, None] * inv[None, :]
    cos, sin = jnp.cos(ang), jnp.sin(ang)
    zero = jnp.zeros_like(sin)
    cos128 = jnp.tile(cos, (1, 4))
    sa = jnp.tile(jnp.concatenate([-sin, zero], axis=1), (1, 2))
    sb = jnp.tile(jnp.concatenate([zero, sin], axis=1), (1, 2))
    return cos128, sa, sb


def _inproj(x3, g, w_in):
    b, t, _ = x3.shape
    cos, sa, sb = _rope_tables(t)
    tok = lambda w: pl.BlockSpec((1, TM_PROJ, w), lambda i, j: (i, j, 0))
    tab = pl.BlockSpec((TM_PROJ, LANES), lambda i, j: (j, 0))
    sds = lambda w, dt: jax.ShapeDtypeStruct((b, t, w), dt)
    outs = [(DA_W, BF16)] * 3 + [(NSA_W, BF16)] * 2 + [(KV_W, F32)] * 2 + [(256, BF16)] * 4 \
        + [(256, F32), (D_MODEL, F32), (D_MODEL, F32)]
    return pl.pallas_call(
        _proj_kernel,
        grid=(b, t // TM_PROJ),
        in_specs=[tok(D_MODEL), _resident((1, D_MODEL)), _resident((D_MODEL, _MAIN_W)), tab, tab, tab],
        out_specs=[tok(w) for w, _ in outs],
        out_shape=[sds(w, dt) for w, dt in outs],
        compiler_params=_params("parallel", "parallel"),
        name="inproj",
    )(x3, g.reshape(1, D_MODEL), _proj_weights(w_in), cos, sa, sb)


def _fill_value_slab(va_ref, v):
    va_ref[:, :LANES] = v
    va_ref[:, LANES:] = jnp.ones(v.shape, v.dtype)


def _causal_bias(qb):
    tri = lax.broadcasted_iota(jnp.int32, (qb, qb), 1) <= lax.broadcasted_iota(jnp.int32, (qb, qb), 0)
    return jnp.where(tri, 0.0, NEG_INF)


def _causal_probs(s, diag_bias):
    qb = s.shape[1]
    r0 = s.shape[2] - qb
    sd = s[:, :, r0:] + diag_bias[None]
    m = jnp.max(sd, axis=2, keepdims=True)
    if not r0:
        return jnp.exp2(sd - m).astype(BF16)
    sm = s[:, :, :r0]
    m = jnp.maximum(m, jnp.max(sm, axis=2, keepdims=True))
    return jnp.concatenate([jnp.exp2(sm - m).astype(BF16), jnp.exp2(sd - m).astype(BF16)], axis=2)


def _da_kernel(lam_ref, gain_ref, q_ref, k_ref, v_ref, o_ref, va_ref, *, lam_init):
    t = q_ref.shape[1]
    qb = QB_DA
    _fill_value_slab(va_ref, v_ref[0])
    lp = lam_ref[...]
    lam = (jnp.exp(jnp.sum(lp[0:1] * lp[1:2], axis=1, keepdims=True))
           - jnp.exp(jnp.sum(lp[2:3] * lp[3:4], axis=1, keepdims=True)) + lam_init)
    gain = gain_ref[0] * (1.0 - lam_init)
    lane = lax.broadcasted_iota(jnp.int32, (qb, LANES), 1)
    diag_bias = _causal_bias(qb)

    def scores(i):
        r0 = i * qb
        q = q_ref[0, r0:r0 + qb, :]
        zero = jnp.zeros_like(q)
        qs = jnp.concatenate([jnp.where(lane < HEAD_DIM, q, zero), jnp.where(lane >= HEAD_DIM, q, zero)], axis=0)
        return _dot_nt(qs, k_ref[0, :r0 + qb, :]).reshape(2, qb, r0 + qb)

    order = list(reversed(range(t // qb)))
    pending = [scores(i) for i in order[:LOOKAHEAD]]
    for pos, i in enumerate(order):
        r0 = i * qb
        s = pending.pop(0)
        if pos + LOOKAHEAD < len(order):
            pending.append(scores(order[pos + LOOKAHEAD]))
        p = _causal_probs(s, diag_bias)
        oa = _dot(p.reshape(2 * qb, r0 + qb), va_ref[:r0 + qb, :])
        o = oa[:, :LANES] / oa[:, LANES:]
        o = o[:qb] - lam * o[qb:]
        o_ref[0, r0:r0 + qb, :] = (_rms(o, gain)).astype(o_ref.dtype)


def _diff_attention(q, k, v, da_lambda, head_gain, lam_init):
    b, t, _ = q.shape
    head = pl.BlockSpec((1, t, LANES), lambda i, h: (i, 0, h))
    return pl.pallas_call(
        functools.partial(_da_kernel, lam_init=lam_init),
        grid=(b, DA_HEADS),
        in_specs=[pl.BlockSpec((4, HEAD_DIM), lambda i, h: (0, 0)),
                  pl.BlockSpec((1, 1, LANES), lambda i, h: (h, 0, 0)), head, head, head],
        out_specs=head,
        out_shape=jax.ShapeDtypeStruct((b, t, DA_W), BF16),
        scratch_shapes=[pltpu.VMEM((t, 2 * LANES), BF16)],
        compiler_params=_params("parallel", "parallel"),
        name="diffattn",
    )(da_lambda, head_gain.reshape(DA_HEADS, 1, LANES), q, k, v)


def _cmp_kernel(k_ref, v_ref, pos_ref, w1_ref, w2_ref, o_ref):
    assert NSA_GROUPS * HEAD_DIM == LANES
    half = CMP_STRIDE * HEAD_DIM
    n = k_ref.shape[1] // CMP_STRIDE
    lo_lane = lax.broadcasted_iota(jnp.int32, (n, LANES), 1) < HEAD_DIM
    for kv, x_ref in enumerate((k_ref, v_ref)):
        tok = [x_ref[0, pl.ds(l, n, stride=CMP_STRIDE), :] for l in range(CMP_STRIDE)]
        chunks = [[], []]
        for l in range(0, CMP_STRIDE, 2):
            a, b = tok[l], tok[l + 1]
            chunks[0].append(jnp.where(lo_lane, a, pltpu.roll(b, HEAD_DIM, 1)))
            chunks[1].append(jnp.where(lo_lane, pltpu.roll(a, HEAD_DIM, 1), b))
        for g in range(NSA_GROUPS):
            x = jnp.concatenate(chunks[g], axis=1)
            pos = pos_ref[kv]
            xa = (x + pos[:, :half]).astype(BF16)
            xb = (x + pos[:, half:]).astype(BF16)
            a = _dot(xa, w1_ref[kv, :half, :])
            bm = _dot(xb, w1_ref[kv, half:, :])
            hid = a + pltpu.roll(bm, bm.shape[0] - 1, 0)
            act = jax.nn.gelu(hid).astype(BF16)
            o_ref[0, kv * NSA_GROUPS + g] = _dot(act, w2_ref[kv]).astype(o_ref.dtype)


def _compress(kcr, vcr, cmp_pos, cmp_w1, cmp_w2):
    b, t, _ = kcr.shape
    n = t // CMP_STRIDE
    w2d = jnp.concatenate([cmp_w2, cmp_w2], axis=-1).astype(BF16)
    raw = pl.BlockSpec((1, t, KV_W), lambda i: (i, 0, 0))
    return pl.pallas_call(
        _cmp_kernel,
        grid=(b,),
        in_specs=[raw, raw,
                  pl.BlockSpec((2, 1, CMP_BLOCK * HEAD_DIM), lambda i: (0, 0, 0)),
                  pl.BlockSpec((2, CMP_BLOCK * HEAD_DIM, CMP_HIDDEN), lambda i: (0, 0, 0)),
                  pl.BlockSpec((2, CMP_HIDDEN, LANES), lambda i: (0, 0, 0))],
        out_specs=pl.BlockSpec((1, 2 * NSA_GROUPS, n, LANES), lambda i: (i, 0, 0, 0)),
        out_shape=jax.ShapeDtypeStruct((b, 2 * NSA_GROUPS, n, LANES), BF16),
        compiler_params=_params("parallel"),
        name="compress",
    )(kcr, vcr, cmp_pos.reshape(2, 1, CMP_BLOCK * HEAD_DIM), cmp_w1.astype(BF16), w2d)


def _split3(x):
    hi = x.astype(BF16)
    r1 = x - hi.astype(F32)
    mid = r1.astype(BF16)
    lo = (r1 - mid.astype(F32)).astype(BF16)
    return hi, mid, lo


def _stack_heads(qblk, heads, lo_lane):
    zero = jnp.zeros((qblk.shape[0], LANES), qblk.dtype)
    parts = []
    for j in heads:
        pair = qblk[:, (j // 2) * LANES:(j // 2 + 1) * LANES]
        parts.append(jnp.where(lo_lane if j % 2 == 0 else ~lo_lane, pair, zero))
    return jnp.concatenate(parts, axis=0)


def _sel_kernel(qc_ref, kc_ref, vc_ref, sel_ref, ocmp_ref):
    t = qc_ref.shape[1]
    qb, hp = QB_NSA, NSA_HPG
    n_slc = t // SLC_BLOCK
    lane = lax.broadcasted_iota(jnp.int32, (qb, LANES), 1)
    lo_lane = lane < HEAD_DIM
    bias_lane = (lane & (HEAD_DIM - 1)) < n_slc
    ov_s = lax.broadcasted_iota(jnp.int32, (LANES, LANES), 0)
    ov_n = lax.broadcasted_iota(jnp.int32, (LANES, LANES), 1)
    ov_t = jnp.where(ov_s < n_slc,
                     jnp.where(ov_n * CMP_STRIDE < (ov_s + 1) * SLC_BLOCK,
                               jnp.where(ov_n * CMP_STRIDE + CMP_BLOCK > ov_s * SLC_BLOCK, 1.0, 0.0), 0.0),
                     0.0).astype(BF16)
    s_idx = lax.broadcasted_iota(jnp.int32, (n_slc, qb), 0)
    t_off = lax.broadcasted_iota(jnp.int32, (n_slc, qb), 1)
    pad = jnp.zeros((HEAD_DIM - n_slc, qb), F32)
    vc_t = vc_ref[0, 0].astype(F32).T[:HEAD_DIM].astype(BF16)

    for i in range(t // qb):
        r0 = i * qb
        rows = slice(r0, r0 + qb)
        n_any = min(LANES, (r0 + qb) // CMP_STRIDE)
        n_all = (max(r0 - (CMP_BLOCK - 1) + CMP_STRIDE, 0) // CMP_STRIDE) // 8 * 8
        q_c = _stack_heads(qc_ref[0, rows, :], range(hp), lo_lane)
        s = _dot_nt(kc_ref[0, 0, :n_any, :], q_c)
        blk = n_all + lax.broadcasted_iota(jnp.int32, (n_any - n_all, hp * qb), 0)
        tq = r0 + (lax.broadcasted_iota(jnp.int32, (n_any - n_all, hp * qb), 1) & (qb - 1))
        valid = (blk * CMP_STRIDE + (CMP_BLOCK - 1)) <= tq
        edge = s[n_all:] + jnp.where(valid, 0.0, NEG_INF)
        m = jnp.max(edge, axis=0, keepdims=True)
        if n_all:
            m = jnp.maximum(m, jnp.max(s[:n_all], axis=0, keepdims=True))
        e_edge = jnp.exp2(edge - m)
        l = jnp.sum(e_edge, axis=0, keepdims=True)
        if n_all:
            e_all = jnp.exp2(s[:n_all] - m)
            l = l + jnp.sum(e_all, axis=0, keepdims=True)
        inv = 1.0 / l
        p = jnp.where(valid, e_edge * inv, 0.0)
        if n_all:
            p = jnp.concatenate([e_all * inv, p], axis=0)
        pb = p.astype(BF16)
        if n_any < LANES:
            pb = jnp.concatenate([pb, jnp.zeros((LANES - n_any, hp * qb), BF16)], axis=0)
        o_t = _dot(vc_t, pb)
        for pr in range(hp // 2):
            both = jnp.concatenate([o_t[:, (2 * pr) * qb:(2 * pr + 1) * qb],
                                    o_t[:, (2 * pr + 1) * qb:(2 * pr + 2) * qb]], axis=0)
            ocmp_ref[0, rows, pr * LANES:(pr + 1) * LANES] = both.T
        psum = sum(p[:, j * qb:(j + 1) * qb] for j in range(hp))
        if n_any < LANES:
            psum = jnp.concatenate([psum, jnp.zeros((LANES - n_any, qb), F32)], axis=0)
        imp_t = sum(_dot(ov_t, term) for term in _split3(psum))
        blk_t = lax.shift_right_logical(r0 + t_off, SLC_SHIFT)
        causal = s_idx <= blk_t
        n_causal = (r0 + qb) // SLC_BLOCK
        if n_causal <= SLC_TOPN:
            sel_t = jnp.where(causal, 1.0, 0.0)
        else:
            forced = (s_idx == 0) | (s_idx == blk_t) | (s_idx == blk_t - 1)
            score = jnp.where(forced, FORCE_SCORE, jnp.where(causal, imp_t[:n_slc], -1.0))
            rank = jnp.zeros((n_slc, qb), F32)
            for sp in range(n_causal):
                other = score[sp:sp + 1, :]
                tie = jnp.where(s_idx > sp, 1.0, 0.0)
                rank = rank + jnp.where(other > score, 1.0, jnp.where(other == score, tie, 0.0))
            sel_t = jnp.where(causal, jnp.where(rank < SLC_TOPN, 1.0, 0.0), 0.0)
        both = jnp.concatenate([sel_t, pad, sel_t, pad], axis=0).T
        sel_ref[0, 0, rows, :] = jnp.where(bias_lane, jnp.where(both > 0.5, 0.0, NEG_INF), 0.0).astype(BF16)


def _select(qc, kvc):
    b, t, _ = qc.shape
    n = t // CMP_STRIDE
    grp = pl.BlockSpec((1, t, GROUP_W), lambda i, g: (i, 0, g))
    return pl.pallas_call(
        _sel_kernel,
        grid=(b, NSA_GROUPS),
        in_specs=[grp, pl.BlockSpec((1, 1, n, LANES), lambda i, g: (i, g, 0, 0)),
                  pl.BlockSpec((1, 1, n, LANES), lambda i, g: (i, NSA_GROUPS + g, 0, 0))],
        out_specs=[pl.BlockSpec((1, 1, t, LANES), lambda i, g: (i, g, 0, 0)), grp],
        out_shape=[jax.ShapeDtypeStruct((b, NSA_GROUPS, t, LANES), BF16),
                   jax.ShapeDtypeStruct((b, t, NSA_W), F32)],
        compiler_params=_params("parallel", "parallel"),
        name="nsa_sel",
    )(qc, kvc, kvc)


def _nsa_kernel(qr_ref, ocmp_ref, ks_ref, vs_ref, kw_ref, vw_ref, gate_ref, sel_ref, o_ref,
                vsa_ref, vwa_ref, kse_ref):
    t = qr_ref.shape[1]
    qb = QB_NSA
    _fill_value_slab(vsa_ref, vs_ref[0])
    _fill_value_slab(vwa_ref, vw_ref[0])
    lane_t = lax.broadcasted_iota(jnp.int32, (t, LANES), 1)
    key_blk = lax.shift_right_logical(lax.broadcasted_iota(jnp.int32, (t, LANES), 0), SLC_SHIFT)
    onehot = jnp.where((lane_t & (HEAD_DIM - 1)) == key_blk, 1.0, 0.0).astype(BF16)
    kse_ref[...] = jnp.where(lane_t < HEAD_DIM, ks_ref[0], onehot)

    lane = lax.broadcasted_iota(jnp.int32, (qb, LANES), 1)
    lo_lane = lane < HEAD_DIM
    diag_bias = _causal_bias(qb)

    def win_bias(w0, r0):
        n = r0 + qb - w0
        kpos = w0 + lax.broadcasted_iota(jnp.int32, (qb, n), 1)
        qpos = r0 + lax.broadcasted_iota(jnp.int32, (qb, n), 0)
        return jnp.where(kpos <= qpos, jnp.where(kpos > qpos - WIN, 0.0, NEG_INF), NEG_INF)

    steady_win_bias = win_bias(0, WIN)

    def normalised(oa):
        return oa[:, :LANES] / oa[:, LANES:]

    def scores(i, pp):
        r0 = i * qb
        rows = slice(r0, r0 + qb)
        qr = qr_ref[0, rows, pp * LANES:(pp + 1) * LANES]
        zero = jnp.zeros_like(qr)
        selb = sel_ref[0, 0, rows, :]
        q_hi = pltpu.roll(qr.astype(F32), HEAD_DIM, 1).astype(BF16)
        qs = jnp.concatenate([jnp.where(lo_lane, qr, selb), jnp.where(lo_lane, q_hi, selb)], axis=0)
        s_slc = _dot_nt(qs, kse_ref[:r0 + qb, :]).reshape(2, qb, r0 + qb)
        w0 = max(r0 - WIN, 0)
        q2 = jnp.concatenate([jnp.where(lo_lane, qr, zero), jnp.where(lo_lane, zero, qr)], axis=0)
        s_win = _dot_nt(q2, kw_ref[0, w0:r0 + qb, :]).reshape(2, qb, r0 + qb - w0)
        return s_slc, s_win

    order = [(i, pp) for i in reversed(range(t // qb)) for pp in range(PAIRS_PER_STEP)]
    pending = [scores(*o) for o in order[:LOOKAHEAD]]
    for pos, (i, pp) in enumerate(order):
        pair = pl.program_id(2) * PAIRS_PER_STEP + pp
        cols = slice(pp * LANES, (pp + 1) * LANES)
        r0 = i * qb
        rows = slice(r0, r0 + qb)
        s_slc, s_win = pending.pop(0)
        if pos + LOOKAHEAD < len(order):
            pending.append(scores(*order[pos + LOOKAHEAD]))
        p = _causal_probs(s_slc, diag_bias)
        o_slc = normalised(_dot(p.reshape(2 * qb, r0 + qb), vsa_ref[:r0 + qb, :]))
        w0 = max(r0 - WIN, 0)
        sw = s_win + (steady_win_bias if r0 >= WIN else win_bias(w0, r0))[None]
        pw = jnp.exp2(sw - jnp.max(sw, axis=2, keepdims=True)).astype(BF16).reshape(2 * qb, r0 + qb - w0)
        o_win = normalised(_dot(pw, vwa_ref[w0:r0 + qb, :]))

        gate = gate_ref[0, rows, :]
        o_cmp = ocmp_ref[0, rows, cols]
        outs = []
        for par in range(2):
            col = 6 * pair + 3 * par
            g = [jnp.sum(jnp.where(lane == col + br, gate, 0.0), axis=1, keepdims=True) for br in range(3)]
            hr = slice(par * qb, (par + 1) * qb)
            outs.append(g[0] * o_cmp + g[1] * o_slc[hr] + g[2] * o_win[hr])
        o_ref[0, rows, cols] = jnp.where(lo_lane, outs[0], outs[1]).astype(o_ref.dtype)


def _nsa(qr, ocmp, ks2, vs2, kw2, vw2, gns, selb):
    b, t, _ = qr.shape
    steps = PAIRS // PAIRS_PER_STEP
    pair = pl.BlockSpec((1, t, PAIRS_PER_STEP * LANES), lambda i, g, p: (i, 0, g * steps + p))
    grp = pl.BlockSpec((1, t, LANES), lambda i, g, p: (i, 0, g))
    sel = pl.BlockSpec((1, 1, t, LANES), lambda i, g, p: (i, g, 0, 0))
    slab = pltpu.VMEM((t, 2 * LANES), BF16)
    keys = pltpu.VMEM((t, LANES), BF16)
    return pl.pallas_call(
        _nsa_kernel,
        grid=(b, NSA_GROUPS, steps),
        in_specs=[pair, pair, grp, grp, grp, grp, grp, sel],
        out_specs=pair,
        out_shape=jax.ShapeDtypeStruct((b, t, NSA_W), BF16),
        scratch_shapes=[slab, slab, keys],
        compiler_params=_params("parallel", "parallel", "parallel"),
        name="nsa_attn",
    )(qr, ocmp, ks2, vs2, kw2, vw2, gns, selb)


def _merge_ffn_kernel(x_ref, yda_ref, yns_ref, ga_ref, gb_ref, wa_ref, wb_ref, wo_ref,
                      g_ref, w1_ref, w3_ref, w2_ref, fg_ref, o_ref):
    merged = ga_ref[...] * _dot(yda_ref[...], wa_ref[...]) + gb_ref[...] * _dot(yns_ref[...], wb_ref[...])
    x = x_ref[...] + _dot(merged.astype(BF16), wo_ref[...])
    o_ref[...] = _rms(_swiglu_residual(x, g_ref[...], w1_ref, w3_ref, w2_ref), fg_ref[...])


def _merge_ffn(x2, yda, yns, ga, gb, w_da, w_nsa, w_out, g, w1, w3, w2, final_g):
    n = x2.shape[0]
    row = pl.BlockSpec((TM_FFN, D_MODEL), lambda i: (i, 0))
    wspec = _resident((D_MODEL, D_MODEL))
    return pl.pallas_call(
        _merge_ffn_kernel,
        grid=(n // TM_FFN,),
        in_specs=[row] * 5 + [wspec] * 3 + _ffn_specs() + [_resident((1, D_MODEL))],
        out_specs=row,
        out_shape=jax.ShapeDtypeStruct((n, D_MODEL), F32),
        compiler_params=_params("parallel"),
        name="merge_ffn",
    )(x2, yda, yns, ga, gb, w_da.astype(BF16), w_nsa.astype(BF16), w_out.astype(BF16),
      g.reshape(1, D_MODEL), w1.astype(BF16), w3.astype(BF16), w2.astype(BF16), final_g.reshape(1, D_MODEL))


def kernel(x, ffn1_norm, ffn1_w1, ffn1_w3, ffn1_w2, mix_norm, w_in, da_lambda, da_head_norm, cmp_pos, cmp_w1,
           cmp_w2, w_proj_da, w_proj_nsa, w_out, ffn2_norm, ffn2_w1, ffn2_w3, ffn2_w2, final_norm):
    b, t, d = x.shape
    depth = ffn1_norm.shape[0]
    assert depth == 1, "one decoder layer per call"
    lam_init = 0.8 - 0.6 * math.exp(-0.3 * 0)
    x2 = _ffn(x.reshape(b * t, d), ffn1_norm[0], ffn1_w1[0], ffn1_w3[0], ffn1_w2[0])
    (qda, kda, vda, qnr, qnc, kcr, vcr, ks2, vs2, kw2, vw2, gns, ga, gb) = _inproj(
        x2.reshape(b, t, d), mix_norm[0], w_in[0])
    yda = _diff_attention(qda, kda, vda, da_lambda[0], da_head_norm[0], lam_init)
    kvc = _compress(kcr, vcr, cmp_pos[0], cmp_w1[0], cmp_w2[0])
    selb, ocmp = _select(qnc, kvc)
    yns = _nsa(qnr, ocmp, ks2, vs2, kw2, vw2, gns, selb)
    flat = lambda a: a.reshape(b * t, a.shape[-1])
    out = _merge_ffn(x2, flat(yda), flat(yns), flat(ga), flat(gb), w_proj_da[0], w_proj_nsa[0], w_out[0],
                     ffn2_norm[0], ffn2_w1[0], ffn2_w3[0], ffn2_w2[0], final_norm)
    return out.reshape(b, t, d)
```

```python
import functools
import math

import jax
import jax.numpy as jnp
from jax import lax
from jax.experimental import pallas as pl
from jax.experimental.pallas import tpu as pltpu

D_MODEL = 1024
HEAD_DIM = 64
DA_HEADS = 8
NSA_HEADS = 16
NSA_GROUPS = 2
NSA_HPG = NSA_HEADS // NSA_GROUPS
CMP_BLOCK = 32
CMP_STRIDE = 16
CMP_HIDDEN = 256
SLC_BLOCK = 64
SLC_TOPN = 16
WIN = 512
D_FF = 2816
ROPE_THETA = 10000.0
EPS = 1e-6
NEG_INF = -1e30
FORCE_SCORE = 1e9
LOG2E = math.log2(math.e)

LANES = 128
DA_W = DA_HEADS * 2 * HEAD_DIM
NSA_W = NSA_HEADS * HEAD_DIM
GROUP_W = NSA_HPG * HEAD_DIM
PAIRS = NSA_HPG // 2
KV_W = NSA_GROUPS * HEAD_DIM
GATE_W = 3 * NSA_HEADS
GATE_PER_GROUP = 3 * NSA_HPG
SLC_SHIFT = SLC_BLOCK.bit_length() - 1
assert 1 << SLC_SHIFT == SLC_BLOCK

TM_FFN = 512
TM_PROJ = 256
FF_CHUNK = 256
QB_DA = 256
QB_NSA = 256
PAIRS_PER_STEP = 1
LOOKAHEAD = 2
assert QB_NSA & (QB_NSA - 1) == 0
VMEM_LIMIT = 56 * 1024 * 1024

BF16 = jnp.bfloat16
F32 = jnp.float32


def _rms(x, g):
    return x * lax.rsqrt(jnp.mean(x * x, axis=-1, keepdims=True) + EPS) * g


def _dot(a, b):
    return jnp.dot(a, b, preferred_element_type=F32)


def _dot_nt(a, b):
    return lax.dot_general(a, b, (((1,), (1,)), ((), ())), preferred_element_type=F32)


def _resident(shape):
    nd = len(shape)
    return pl.BlockSpec(shape, lambda *_: (0,) * nd, pipeline_mode=pl.Buffered(1))


def _params(*sem):
    return pltpu.CompilerParams(dimension_semantics=sem, vmem_limit_bytes=VMEM_LIMIT)


def _swiglu_residual(x, g, w1_ref, w3_ref, w2_ref):
    h = _rms(x, g).astype(BF16)
    acc = jnp.zeros_like(x)
    for c in range(D_FF // FF_CHUNK):
        sl = slice(c * FF_CHUNK, (c + 1) * FF_CHUNK)
        a = _dot(h, w1_ref[:, sl])
        b = _dot(h, w3_ref[:, sl])
        u = (a * jax.nn.sigmoid(a) * b).astype(BF16)
        acc = acc + _dot(u, w2_ref[sl, :])
    return x + 0.5 * acc


def _ffn_kernel(x_ref, g_ref, w1_ref, w3_ref, w2_ref, o_ref):
    o_ref[...] = _swiglu_residual(x_ref[...], g_ref[...], w1_ref, w3_ref, w2_ref)


def _ffn_specs():
    return [_resident((1, D_MODEL)), _resident((D_MODEL, D_FF)), _resident((D_MODEL, D_FF)),
            _resident((D_FF, D_MODEL))]


def _ffn(x2, g, w1, w3, w2):
    n = x2.shape[0]
    row = pl.BlockSpec((TM_FFN, D_MODEL), lambda i: (i, 0))
    return pl.pallas_call(
        _ffn_kernel,
        grid=(n // TM_FFN,),
        in_specs=[row] + _ffn_specs(),
        out_specs=row,
        out_shape=jax.ShapeDtypeStruct((n, D_MODEL), F32),
        compiler_params=_params("parallel"),
        name="ffn",
    )(x2, g.reshape(1, D_MODEL), w1.astype(BF16), w3.astype(BF16), w2.astype(BF16))


_C_QDA, _C_KDA, _C_VDA, _C_QNS = 0, 1024, 2048, 3072
_C_KCV = 4096
_C_KSW = 4352
_C_VSW = 4608
_C_GNS = 4864
_C_GA = 4992
_C_GB = 6016
_MAIN_W = 7040


def _proj_kernel(x_ref, g_ref, w_ref, cos_ref, sa_ref, sb_ref,
                 qda_ref, kda_ref, vda_ref, qnr_ref, qnc_ref, kcr_ref, vcr_ref,
                 ksw_ref, vsw_ref, gns_ref, ga_ref, gb_ref):
    h = _rms(x_ref[0], g_ref[...]).astype(BF16)
    cos, sa, sb = cos_ref[...], sa_ref[...], sb_ref[...]

    def proj(c0, width):
        return _dot(h, w_ref[:, c0:c0 + width])

    def rope(y):
        cols = []
        for c in range(y.shape[1] // LANES):
            v = y[:, c * LANES:(c + 1) * LANES]
            cols.append(v * cos + pltpu.roll(v, LANES - 32, 1) * sa + pltpu.roll(v, 32, 1) * sb)
        return cols[0] if len(cols) == 1 else jnp.concatenate(cols, axis=1)

    qscale = HEAD_DIM ** -0.5 * LOG2E
    half = DA_W // 2
    for c0 in (0, half):
        qda_ref[0, :, c0:c0 + half] = (rope(proj(_C_QDA + c0, half)) * qscale).astype(BF16)
        kda_ref[0, :, c0:c0 + half] = rope(proj(_C_KDA + c0, half)).astype(BF16)
        vda_ref[0, :, c0:c0 + half] = proj(_C_VDA + c0, half).astype(BF16)
        qn = proj(_C_QNS + c0, half)
        qnc_ref[0, :, c0:c0 + half] = (qn * qscale).astype(BF16)
        qnr_ref[0, :, c0:c0 + half] = (rope(qn) * qscale).astype(BF16)
        ga_ref[0, :, c0:c0 + half] = jax.nn.sigmoid(proj(_C_GA + c0, half))
        gb_ref[0, :, c0:c0 + half] = jax.nn.sigmoid(proj(_C_GB + c0, half))
    kcv = proj(_C_KCV, 256)
    kcr_ref[0] = kcv[:, :KV_W]
    vcr_ref[0] = kcv[:, KV_W:]
    ksw_ref[0] = rope(proj(_C_KSW, 256)).astype(BF16)
    vsw_ref[0] = proj(_C_VSW, 256).astype(BF16)
    gns_ref[0] = jax.nn.sigmoid(proj(_C_GNS, LANES))


def _proj_weights(w_in):
    offs = [0]
    for w in (DA_W, DA_W, DA_W, NSA_W) + (KV_W,) * 6 + (GATE_W, D_MODEL, D_MODEL):
        offs.append(offs[-1] + w)
    (o_q, o_k, o_v, o_qn, o_kc, o_vc, o_ks, o_vs, o_kw, o_vw, o_g, o_ga, o_gb, _) = offs

    def paired(o_a, o_b):
        parts = []
        for g in range(NSA_GROUPS):
            parts += [w_in[:, o_a + g * HEAD_DIM:o_a + (g + 1) * HEAD_DIM],
                      w_in[:, o_b + g * HEAD_DIM:o_b + (g + 1) * HEAD_DIM]]
        return jnp.concatenate(parts, axis=1)

    gates = jnp.pad(w_in[:, o_g:o_ga], ((0, 0), (0, LANES - GATE_W)))
    slab = jnp.concatenate(
        [w_in[:, o_q:o_kc], w_in[:, o_kc:o_ks], paired(o_ks, o_kw), paired(o_vs, o_vw), gates,
         w_in[:, o_ga:o_gb], w_in[:, o_gb:]], axis=1)
    return slab.astype(BF16)


def _rope_tables(t):
    pos = jnp.arange(t, dtype=F32)
    inv = 1.0 / (ROPE_THETA ** (jnp.arange(0, HEAD_DIM, 2, dtype=F32) / HEAD_DIM))
    ang = pos[:, None] * inv[None, :]
    cos, sin = jnp.cos(ang), jnp.sin(ang)
    zero = jnp.zeros_like(sin)
    cos128 = jnp.tile(cos, (1, 4))
    sa = jnp.tile(jnp.concatenate([-sin, zero], axis=1), (1, 2))
    sb = jnp.tile(jnp.concatenate([zero, sin], axis=1), (1, 2))
    return cos128, sa, sb


def _inproj(x3, g, w_in):
    b, t, _ = x3.shape
    cos, sa, sb = _rope_tables(t)
    tok = lambda w: pl.BlockSpec((1, TM_PROJ, w), lambda i, j: (i, j, 0))
    tab = pl.BlockSpec((TM_PROJ, LANES), lambda i, j: (j, 0))
    sds = lambda w, dt: jax.ShapeDtypeStruct((b, t, w), dt)
    outs = [(DA_W, BF16)] * 3 + [(NSA_W, BF16)] * 2 + [(KV_W, F32)] * 2 + [(256, BF16)] * 2 \
        + [(LANES, F32), (D_MODEL, F32), (D_MODEL, F32)]
    return pl.pallas_call(
        _proj_kernel,
        grid=(b, t // TM_PROJ),
        in_specs=[tok(D_MODEL), _resident((1, D_MODEL)), _resident((D_MODEL, _MAIN_W)), tab, tab, tab],
        out_specs=[tok(w) for w, _ in outs],
        out_shape=[sds(w, dt) for w, dt in outs],
        compiler_params=_params("parallel", "parallel"),
        name="inproj",
    )(x3, g.reshape(1, D_MODEL), _proj_weights(w_in), cos, sa, sb)


def _fill_value_slab(va_ref, v):
    va_ref[:, :LANES] = v
    va_ref[:, LANES:] = jnp.ones(v.shape, v.dtype)


def _causal_bias(qb):
    tri = lax.broadcasted_iota(jnp.int32, (qb, qb), 1) <= lax.broadcasted_iota(jnp.int32, (qb, qb), 0)
    return jnp.where(tri, 0.0, NEG_INF)


def _causal_probs(s, diag_bias):
    qb = s.shape[1]
    r0 = s.shape[2] - qb
    sd = s[:, :, r0:] + diag_bias[None]
    m = jnp.max(sd, axis=2, keepdims=True)
    if not r0:
        return jnp.exp2(sd - m).astype(BF16)
    sm = s[:, :, :r0]
    m = jnp.maximum(m, jnp.max(sm, axis=2, keepdims=True))
    return jnp.concatenate([jnp.exp2(sm - m).astype(BF16), jnp.exp2(sd - m).astype(BF16)], axis=2)


def _da_kernel(lam_ref, gain_ref, q_ref, k_ref, v_ref, o_ref, va_ref, *, lam_init):
    t = q_ref.shape[1]
    qb = QB_DA
    _fill_value_slab(va_ref, v_ref[0])
    lp = lam_ref[...]
    lam = (jnp.exp(jnp.sum(lp[0:1] * lp[1:2], axis=1, keepdims=True))
           - jnp.exp(jnp.sum(lp[2:3] * lp[3:4], axis=1, keepdims=True)) + lam_init)
    gain = gain_ref[0] * (1.0 - lam_init)
    lane = lax.broadcasted_iota(jnp.int32, (qb, LANES), 1)
    diag_bias = _causal_bias(qb)

    def scores(i):
        r0 = i * qb
        q = q_ref[0, r0:r0 + qb, :]
        zero = jnp.zeros_like(q)
        qs = jnp.concatenate([jnp.where(lane < HEAD_DIM, q, zero), jnp.where(lane >= HEAD_DIM, q, zero)], axis=0)
        return _dot_nt(qs, k_ref[0, :r0 + qb, :]).reshape(2, qb, r0 + qb)

    order = list(reversed(range(t // qb)))
    pending = [scores(i) for i in order[:LOOKAHEAD]]
    for pos, i in enumerate(order):
        r0 = i * qb
        s = pending.pop(0)
        if pos + LOOKAHEAD < len(order):
            pending.append(scores(order[pos + LOOKAHEAD]))
        p = _causal_probs(s, diag_bias)
        oa = _dot(p.reshape(2 * qb, r0 + qb), va_ref[:r0 + qb, :])
        o = oa[:, :LANES] / oa[:, LANES:]
        o = o[:qb] - lam * o[qb:]
        o_ref[0, r0:r0 + qb, :] = (_rms(o, gain)).astype(o_ref.dtype)


def _diff_attention(q, k, v, da_lambda, head_gain, lam_init):
    b, t, _ = q.shape
    head = pl.BlockSpec((1, t, LANES), lambda i, h: (i, 0, h))
    return pl.pallas_call(
        functools.partial(_da_kernel, lam_init=lam_init),
        grid=(b, DA_HEADS),
        in_specs=[pl.BlockSpec((4, HEAD_DIM), lambda i, h: (0, 0)),
                  pl.BlockSpec((1, 1, LANES), lambda i, h: (h, 0, 0)), head, head, head],
        out_specs=head,
        out_shape=jax.ShapeDtypeStruct((b, t, DA_W), BF16),
        scratch_shapes=[pltpu.VMEM((t, 2 * LANES), BF16)],
        compiler_params=_params("parallel", "parallel"),
        name="diffattn",
    )(da_lambda, head_gain.reshape(DA_HEADS, 1, LANES), q, k, v)


def _cmp_kernel(k_ref, v_ref, pos_ref, w1_ref, w2_ref, o_ref):
    assert NSA_GROUPS * HEAD_DIM == LANES
    half = CMP_STRIDE * HEAD_DIM
    n = k_ref.shape[1] // CMP_STRIDE
    lo_lane = lax.broadcasted_iota(jnp.int32, (n, LANES), 1) < HEAD_DIM
    for kv, x_ref in enumerate((k_ref, v_ref)):
        tok = [x_ref[0, pl.ds(l, n, stride=CMP_STRIDE), :] for l in range(CMP_STRIDE)]
        chunks = [[], []]
        for l in range(0, CMP_STRIDE, 2):
            a, b = tok[l], tok[l + 1]
            chunks[0].append(jnp.where(lo_lane, a, pltpu.roll(b, HEAD_DIM, 1)))
            chunks[1].append(jnp.where(lo_lane, pltpu.roll(a, HEAD_DIM, 1), b))
        for g in range(NSA_GROUPS):
            x = jnp.concatenate(chunks[g], axis=1)
            pos = pos_ref[kv]
            xa = (x + pos[:, :half]).astype(BF16)
            xb = (x + pos[:, half:]).astype(BF16)
            a = _dot(xa, w1_ref[kv, :half, :])
            bm = _dot(xb, w1_ref[kv, half:, :])
            hid = a + pltpu.roll(bm, bm.shape[0] - 1, 0)
            act = jax.nn.gelu(hid).astype(BF16)
            o_ref[0, kv * NSA_GROUPS + g] = _dot(act, w2_ref[kv]).astype(o_ref.dtype)


def _compress(kcr, vcr, cmp_pos, cmp_w1, cmp_w2):
    b, t, _ = kcr.shape
    n = t // CMP_STRIDE
    w2d = jnp.concatenate([cmp_w2, cmp_w2], axis=-1).astype(BF16)
    raw = pl.BlockSpec((1, t, KV_W), lambda i: (i, 0, 0))
    return pl.pallas_call(
        _cmp_kernel,
        grid=(b,),
        in_specs=[raw, raw,
                  pl.BlockSpec((2, 1, CMP_BLOCK * HEAD_DIM), lambda i: (0, 0, 0)),
                  pl.BlockSpec((2, CMP_BLOCK * HEAD_DIM, CMP_HIDDEN), lambda i: (0, 0, 0)),
                  pl.BlockSpec((2, CMP_HIDDEN, LANES), lambda i: (0, 0, 0))],
        out_specs=pl.BlockSpec((1, 2 * NSA_GROUPS, n, LANES), lambda i: (i, 0, 0, 0)),
        out_shape=jax.ShapeDtypeStruct((b, 2 * NSA_GROUPS, n, LANES), BF16),
        compiler_params=_params("parallel"),
        name="compress",
    )(kcr, vcr, cmp_pos.reshape(2, 1, CMP_BLOCK * HEAD_DIM), cmp_w1.astype(BF16), w2d)


def _split3(x):
    hi = x.astype(BF16)
    r1 = x - hi.astype(F32)
    mid = r1.astype(BF16)
    lo = (r1 - mid.astype(F32)).astype(BF16)
    return hi, mid, lo


def _stack_heads(qblk, heads, lo_lane):
    zero = jnp.zeros((qblk.shape[0], LANES), qblk.dtype)
    parts = []
    for j in heads:
        pair = qblk[:, (j // 2) * LANES:(j // 2 + 1) * LANES]
        parts.append(jnp.where(lo_lane if j % 2 == 0 else ~lo_lane, pair, zero))
    return jnp.concatenate(parts, axis=0)


def _sel_kernel(qc_ref, kc_ref, vc_ref, sel_ref, ocmp_ref):
    t = qc_ref.shape[1]
    qb, hp = QB_NSA, NSA_HPG
    n_slc = t // SLC_BLOCK
    lane = lax.broadcasted_iota(jnp.int32, (qb, LANES), 1)
    lo_lane = lane < HEAD_DIM
    bias_lane = (lane & (HEAD_DIM - 1)) < n_slc
    ov_s = lax.broadcasted_iota(jnp.int32, (LANES, LANES), 0)
    ov_n = lax.broadcasted_iota(jnp.int32, (LANES, LANES), 1)
    ov_t = jnp.where(ov_s < n_slc,
                     jnp.where(ov_n * CMP_STRIDE < (ov_s + 1) * SLC_BLOCK,
                               jnp.where(ov_n * CMP_STRIDE + CMP_BLOCK > ov_s * SLC_BLOCK, 1.0, 0.0), 0.0),
                     0.0).astype(BF16)
    s_idx = lax.broadcasted_iota(jnp.int32, (n_slc, qb), 0)
    t_off = lax.broadcasted_iota(jnp.int32, (n_slc, qb), 1)
    pad = jnp.zeros((HEAD_DIM - n_slc, qb), F32)
    vc_t = vc_ref[0, 0].astype(F32).T[:HEAD_DIM].astype(BF16)

    for i in range(t // qb):
        r0 = i * qb
        rows = slice(r0, r0 + qb)
        n_any = min(LANES, (r0 + qb) // CMP_STRIDE)
        n_all = (max(r0 - (CMP_BLOCK - 1) + CMP_STRIDE, 0) // CMP_STRIDE) // 8 * 8
        q_c = _stack_heads(qc_ref[0, rows, :], range(hp), lo_lane)
        s = _dot_nt(kc_ref[0, 0, :n_any, :], q_c)
        blk = n_all + lax.broadcasted_iota(jnp.int32, (n_any - n_all, hp * qb), 0)
        tq = r0 + (lax.broadcasted_iota(jnp.int32, (n_any - n_all, hp * qb), 1) & (qb - 1))
        valid = (blk * CMP_STRIDE + (CMP_BLOCK - 1)) <= tq
        edge = s[n_all:] + jnp.where(valid, 0.0, NEG_INF)
        m = jnp.max(edge, axis=0, keepdims=True)
        if n_all:
            m = jnp.maximum(m, jnp.max(s[:n_all], axis=0, keepdims=True))
        e_edge = jnp.exp2(edge - m)
        l = jnp.sum(e_edge, axis=0, keepdims=True)
        if n_all:
            e_all = jnp.exp2(s[:n_all] - m)
            l = l + jnp.sum(e_all, axis=0, keepdims=True)
        inv = 1.0 / l
        p = jnp.where(valid, e_edge * inv, 0.0)
        if n_all:
            p = jnp.concatenate([e_all * inv, p], axis=0)
        pb = p.astype(BF16)
        if n_any < LANES:
            pb = jnp.concatenate([pb, jnp.zeros((LANES - n_any, hp * qb), BF16)], axis=0)
        o_t = _dot(vc_t, pb)
        for pr in range(hp // 2):
            both = jnp.concatenate([o_t[:, (2 * pr) * qb:(2 * pr + 1) * qb],
                                    o_t[:, (2 * pr + 1) * qb:(2 * pr + 2) * qb]], axis=0)
            ocmp_ref[0, rows, pr * LANES:(pr + 1) * LANES] = both.T
        psum = sum(p[:, j * qb:(j + 1) * qb] for j in range(hp))
        if n_any < LANES:
            psum = jnp.concatenate([psum, jnp.zeros((LANES - n_any, qb), F32)], axis=0)
        imp_t = sum(_dot(ov_t, term) for term in _split3(psum))
        blk_t = lax.shift_right_logical(r0 + t_off, SLC_SHIFT)
        causal = s_idx <= blk_t
        n_causal = (r0 + qb) // SLC_BLOCK
        if n_causal <= SLC_TOPN:
            sel_t = jnp.where(causal, 1.0, 0.0)
        else:
            forced = (s_idx == 0) | (s_idx == blk_t) | (s_idx == blk_t - 1)
            score = jnp.where(forced, FORCE_SCORE, jnp.where(causal, imp_t[:n_slc], -1.0))
            rank = jnp.zeros((n_slc, qb), F32)
            for sp in range(n_causal):
                other = score[sp:sp + 1, :]
                tie = jnp.where(s_idx > sp, 1.0, 0.0)
                rank = rank + jnp.where(other > score, 1.0, jnp.where(other == score, tie, 0.0))
            sel_t = jnp.where(causal, jnp.where(rank < SLC_TOPN, 1.0, 0.0), 0.0)
        both = jnp.concatenate([sel_t, pad, sel_t, pad], axis=0).T
        sel_ref[0, 0, rows, :] = jnp.where(bias_lane, jnp.where(both > 0.5, 0.0, NEG_INF), 0.0).astype(BF16)


def _select(qc, kvc):
    b, t, _ = qc.shape
    n = t // CMP_STRIDE
    grp = pl.BlockSpec((1, t, GROUP_W), lambda i, g: (i, 0, g))
    return pl.pallas_call(
        _sel_kernel,
        grid=(b, NSA_GROUPS),
        in_specs=[grp, pl.BlockSpec((1, 1, n, LANES), lambda i, g: (i, g, 0, 0)),
                  pl.BlockSpec((1, 1, n, LANES), lambda i, g: (i, NSA_GROUPS + g, 0, 0))],
        out_specs=[pl.BlockSpec((1, 1, t, LANES), lambda i, g: (i, g, 0, 0)), grp],
        out_shape=[jax.ShapeDtypeStruct((b, NSA_GROUPS, t, LANES), BF16),
                   jax.ShapeDtypeStruct((b, t, NSA_W), F32)],
        compiler_params=_params("parallel", "parallel"),
        name="nsa_sel",
    )(qc, kvc, kvc)


def _swap_halves(x):
    return pltpu.roll(x.astype(F32), HEAD_DIM, 1).astype(x.dtype)


def _nsa_kernel(qr_ref, ocmp_ref, ksw_ref, vsw_ref, gate_ref, sel_ref, o_ref, vsa_ref, vwa_ref, kse_ref):
    t = qr_ref.shape[1]
    qb = QB_NSA
    group = pl.program_id(1)

    lane_t = lax.broadcasted_iota(jnp.int32, (t, LANES), 1)
    lo_t = lane_t < HEAD_DIM
    vsw = vsw_ref[0]
    vws = _swap_halves(vsw)
    _fill_value_slab(vsa_ref, jnp.where(lo_t, vsw, vws))
    _fill_value_slab(vwa_ref, jnp.where(lo_t, vws, vsw))
    key_blk = lax.shift_right_logical(lax.broadcasted_iota(jnp.int32, (t, LANES), 0), SLC_SHIFT)
    onehot = jnp.where((lane_t & (HEAD_DIM - 1)) == key_blk, 1.0, 0.0).astype(BF16)
    kse_ref[...] = jnp.where(lo_t, ksw_ref[0], onehot)

    lane = lax.broadcasted_iota(jnp.int32, (qb, LANES), 1)
    lo_lane = lane < HEAD_DIM
    diag_bias = _causal_bias(qb)

    def win_bias(w0, r0):
        n = r0 + qb - w0
        kpos = w0 + lax.broadcasted_iota(jnp.int32, (qb, n), 1)
        qpos = r0 + lax.broadcasted_iota(jnp.int32, (qb, n), 0)
        return jnp.where(kpos <= qpos, jnp.where(kpos > qpos - WIN, 0.0, NEG_INF), NEG_INF)

    steady_win_bias = win_bias(0, WIN)

    def normalised(oa):
        return oa[:, :LANES] / oa[:, LANES:]

    def scores(i, pp):
        r0 = i * qb
        rows = slice(r0, r0 + qb)
        qr = qr_ref[0, rows, pp * LANES:(pp + 1) * LANES]
        zero = jnp.zeros_like(qr)
        selb = sel_ref[0, 0, rows, :]
        q_sw = _swap_halves(qr)
        qs = jnp.concatenate([jnp.where(lo_lane, qr, selb), jnp.where(lo_lane, q_sw, selb)], axis=0)
        s_slc = _dot_nt(qs, kse_ref[:r0 + qb, :]).reshape(2, qb, r0 + qb)
        w0 = max(r0 - WIN, 0)
        q2 = jnp.concatenate([jnp.where(lo_lane, zero, q_sw), jnp.where(lo_lane, zero, qr)], axis=0)
        s_win = _dot_nt(q2, ksw_ref[0, w0:r0 + qb, :]).reshape(2, qb, r0 + qb - w0)
        return s_slc, s_win

    order = [(i, pp) for i in reversed(range(t // qb)) for pp in range(PAIRS_PER_STEP)]
    pending = [scores(*o) for o in order[:LOOKAHEAD]]
    for pos, (i, pp) in enumerate(order):
        pair = pl.program_id(2) * PAIRS_PER_STEP + pp
        cols = slice(pp * LANES, (pp + 1) * LANES)
        r0 = i * qb
        rows = slice(r0, r0 + qb)
        s_slc, s_win = pending.pop(0)
        if pos + LOOKAHEAD < len(order):
            pending.append(scores(*order[pos + LOOKAHEAD]))
        p = _causal_probs(s_slc, diag_bias)
        o_slc = normalised(_dot(p.reshape(2 * qb, r0 + qb), vsa_ref[:r0 + qb, :]))
        w0 = max(r0 - WIN, 0)
        sw = s_win + (steady_win_bias if r0 >= WIN else win_bias(w0, r0))[None]
        pw = jnp.exp2(sw - jnp.max(sw, axis=2, keepdims=True)).astype(BF16).reshape(2 * qb, r0 + qb - w0)
        o_win = normalised(_dot(pw, vwa_ref[w0:r0 + qb, :]))

        gate = gate_ref[0, rows, :]
        o_cmp = ocmp_ref[0, rows, cols]
        outs = []
        for par in range(2):
            col = GATE_PER_GROUP * group + 6 * pair + 3 * par
            g = [jnp.sum(jnp.where(lane == col + br, gate, 0.0), axis=1, keepdims=True) for br in range(3)]
            hr = slice(par * qb, (par + 1) * qb)
            outs.append(g[0] * o_cmp + g[1] * o_slc[hr] + g[2] * o_win[hr])
        o_ref[0, rows, cols] = jnp.where(lo_lane, outs[0], outs[1]).astype(o_ref.dtype)


def _nsa(qr, ocmp, ksw, vsw, gns, selb):
    b, t, _ = qr.shape
    steps = PAIRS // PAIRS_PER_STEP
    pair = pl.BlockSpec((1, t, PAIRS_PER_STEP * LANES), lambda i, g, p: (i, 0, g * steps + p))
    grp = pl.BlockSpec((1, t, LANES), lambda i, g, p: (i, 0, g))
    gates = pl.BlockSpec((1, t, LANES), lambda i, g, p: (i, 0, 0))
    sel = pl.BlockSpec((1, 1, t, LANES), lambda i, g, p: (i, g, 0, 0))
    slab = pltpu.VMEM((t, 2 * LANES), BF16)
    keys = pltpu.VMEM((t, LANES), BF16)
    return pl.pallas_call(
        _nsa_kernel,
        grid=(b, NSA_GROUPS, steps),
        in_specs=[pair, pair, grp, grp, gates, sel],
        out_specs=pair,
        out_shape=jax.ShapeDtypeStruct((b, t, NSA_W), BF16),
        scratch_shapes=[slab, slab, keys],
        compiler_params=_params("parallel", "parallel", "parallel"),
        name="nsa_attn",
    )(qr, ocmp, ksw, vsw, gns, selb)


def _merge_ffn_kernel(x_ref, yda_ref, yns_ref, ga_ref, gb_ref, wa_ref, wb_ref, wo_ref,
                      g_ref, w1_ref, w3_ref, w2_ref, fg_ref, o_ref):
    merged = ga_ref[...] * _dot(yda_ref[...], wa_ref[...]) + gb_ref[...] * _dot(yns_ref[...], wb_ref[...])
    x = x_ref[...] + _dot(merged.astype(BF16), wo_ref[...])
    o_ref[...] = _rms(_swiglu_residual(x, g_ref[...], w1_ref, w3_ref, w2_ref), fg_ref[...])


def _merge_ffn(x2, yda, yns, ga, gb, w_da, w_nsa, w_out, g, w1, w3, w2, final_g):
    n = x2.shape[0]
    row = pl.BlockSpec((TM_FFN, D_MODEL), lambda i: (i, 0))
    wspec = _resident((D_MODEL, D_MODEL))
    return pl.pallas_call(
        _merge_ffn_kernel,
        grid=(n // TM_FFN,),
        in_specs=[row] * 5 + [wspec] * 3 + _ffn_specs() + [_resident((1, D_MODEL))],
        out_specs=row,
        out_shape=jax.ShapeDtypeStruct((n, D_MODEL), F32),
        compiler_params=_params("parallel"),
        name="merge_ffn",
    )(x2, yda, yns, ga, gb, w_da.astype(BF16), w_nsa.astype(BF16), w_out.astype(BF16),
      g.reshape(1, D_MODEL), w1.astype(BF16), w3.astype(BF16), w2.astype(BF16), final_g.reshape(1, D_MODEL))


def kernel(x, ffn1_norm, ffn1_w1, ffn1_w3, ffn1_w2, mix_norm, w_in, da_lambda, da_head_norm, cmp_pos, cmp_w1,
           cmp_w2, w_proj_da, w_proj_nsa, w_out, ffn2_norm, ffn2_w1, ffn2_w3, ffn2_w2, final_norm):
    b, t, d = x.shape
    depth = ffn1_norm.shape[0]
    assert depth == 1, "one decoder layer per call"
    lam_init = 0.8 - 0.6 * math.exp(-0.3 * 0)
    x2 = _ffn(x.reshape(b * t, d), ffn1_norm[0], ffn1_w1[0], ffn1_w3[0], ffn1_w2[0])
    (qda, kda, vda, qnr, qnc, kcr, vcr, ksw, vsw, gns, ga, gb) = _inproj(
        x2.reshape(b, t, d), mix_norm[0], w_in[0])
    yda = _diff_attention(qda, kda, vda, da_lambda[0], da_head_norm[0], lam_init)
    kvc = _compress(kcr, vcr, cmp_pos[0], cmp_w1[0], cmp_w2[0])
    selb, ocmp = _select(qnc, kvc)
    yns = _nsa(qnr, ocmp, ksw, vsw, gns, selb)
    flat = lambda a: a.reshape(b * t, a.shape[-1])
    out = _merge_ffn(x2, flat(yda), flat(yns), flat(ga), flat(gb), w_proj_da[0], w_proj_nsa[0], w_out[0],
                     ffn2_norm[0], ffn2_w1[0], ffn2_w3[0], ffn2_w2[0], final_norm)
    return out.reshape(b, t, d)
```

```python
import functools
import math

import jax
import jax.numpy as jnp
from jax import lax
from jax.experimental import pallas as pl
from jax.experimental.pallas import tpu as pltpu

D_MODEL = 1024
HEAD_DIM = 64
DA_HEADS = 8
NSA_HEADS = 16
NSA_GROUPS = 2
NSA_HPG = NSA_HEADS // NSA_GROUPS
CMP_BLOCK = 32
CMP_STRIDE = 16
CMP_HIDDEN = 256
SLC_BLOCK = 64
SLC_TOPN = 16
WIN = 512
D_FF = 2816
ROPE_THETA = 10000.0
EPS = 1e-6
NEG_INF = -1e30
FORCE_SCORE = 1e9
LOG2E = math.log2(math.e)

LANES = 128
DA_W = DA_HEADS * 2 * HEAD_DIM
NSA_W = NSA_HEADS * HEAD_DIM
GROUP_W = NSA_HPG * HEAD_DIM
PAIRS = NSA_HPG // 2
KV_W = NSA_GROUPS * HEAD_DIM
GATE_W = 3 * NSA_HEADS
GATE_PER_GROUP = 3 * NSA_HPG
SLC_SHIFT = SLC_BLOCK.bit_length() - 1
assert 1 << SLC_SHIFT == SLC_BLOCK

TM_FFN = 512
TM_PROJ = 256
FF_CHUNK = 256
QB_DA = 256
QB_NSA = 256
PAIRS_PER_STEP = 1
LOOKAHEAD = 2
assert QB_NSA & (QB_NSA - 1) == 0
VMEM_LIMIT = 56 * 1024 * 1024

BF16 = jnp.bfloat16
F32 = jnp.float32


def _rms(x, g):
    return x * lax.rsqrt(jnp.mean(x * x, axis=-1, keepdims=True) + EPS) * g


def _dot(a, b):
    return jnp.dot(a, b, preferred_element_type=F32)


def _dot_nt(a, b):
    return lax.dot_general(a, b, (((1,), (1,)), ((), ())), preferred_element_type=F32)


def _resident(shape):
    nd = len(shape)
    return pl.BlockSpec(shape, lambda *_: (0,) * nd, pipeline_mode=pl.Buffered(1))


def _params(*sem):
    return pltpu.CompilerParams(dimension_semantics=sem, vmem_limit_bytes=VMEM_LIMIT)


def _swiglu_residual(x, g, w1_ref, w3_ref, w2_ref):
    h = _rms(x, g).astype(BF16)
    acc = jnp.zeros_like(x)
    for c in range(D_FF // FF_CHUNK):
        sl = slice(c * FF_CHUNK, (c + 1) * FF_CHUNK)
        a = _dot(h, w1_ref[:, sl])
        b = _dot(h, w3_ref[:, sl])
        u = (a * jax.nn.sigmoid(a) * b).astype(BF16)
        acc = acc + _dot(u, w2_ref[sl, :])
    return x + 0.5 * acc


def _ffn_kernel(x_ref, g_ref, w1_ref, w3_ref, w2_ref, o_ref):
    o_ref[...] = _swiglu_residual(x_ref[...], g_ref[...], w1_ref, w3_ref, w2_ref)


def _ffn_specs():
    return [_resident((1, D_MODEL)), _resident((D_MODEL, D_FF)), _resident((D_MODEL, D_FF)),
            _resident((D_FF, D_MODEL))]


def _ffn(x2, g, w1, w3, w2):
    n = x2.shape[0]
    row = pl.BlockSpec((TM_FFN, D_MODEL), lambda i: (i, 0))
    return pl.pallas_call(
        _ffn_kernel,
        grid=(n // TM_FFN,),
        in_specs=[row] + _ffn_specs(),
        out_specs=row,
        out_shape=jax.ShapeDtypeStruct((n, D_MODEL), F32),
        compiler_params=_params("parallel"),
        name="ffn",
    )(x2, g.reshape(1, D_MODEL), w1.astype(BF16), w3.astype(BF16), w2.astype(BF16))


_C_QDA, _C_KDA, _C_VDA, _C_QNS = 0, DA_W, 2 * DA_W, 3 * DA_W
_C_KCV = 3 * DA_W + NSA_W
_HEAD_W = _C_KCV + 2 * KV_W
_M_KSW = 0
_M_VSW = 2 * KV_W
_M_GNS = 4 * KV_W
_MID_W = 4 * KV_W + LANES
_G_GA, _G_GB = 0, D_MODEL


def _proj_kernel(x_ref, g_ref, wh_ref, wm_ref, wg_ref, cos_ref, sa_ref, sb_ref,
                 qda_ref, kda_ref, vda_ref, qnr_ref, qnc_ref, kcr_ref, vcr_ref,
                 ksw_ref, vsw_ref, gns_ref, ga_ref, gb_ref):
    h = _rms(x_ref[0], g_ref[...]).astype(BF16)
    cos, sa, sb = cos_ref[...], sa_ref[...], sb_ref[...]

    def proj(c0, width, w_ref=wh_ref):
        return _dot(h, w_ref[:, c0:c0 + width])

    def rope(y):
        cols = []
        for c in range(y.shape[1] // LANES):
            v = y[:, c * LANES:(c + 1) * LANES]
            cols.append(v * cos + pltpu.roll(v, LANES - 32, 1) * sa + pltpu.roll(v, 32, 1) * sb)
        return cols[0] if len(cols) == 1 else jnp.concatenate(cols, axis=1)

    qscale = HEAD_DIM ** -0.5 * LOG2E
    half = DA_W // 2
    for c0 in (0, half):
        qda_ref[0, :, c0:c0 + half] = (rope(proj(_C_QDA + c0, half)) * qscale).astype(BF16)
        kda_ref[0, :, c0:c0 + half] = rope(proj(_C_KDA + c0, half)).astype(BF16)
        vda_ref[0, :, c0:c0 + half] = proj(_C_VDA + c0, half).astype(BF16)
        qn = proj(_C_QNS + c0, half)
        qnc_ref[0, :, c0:c0 + half] = (qn * qscale).astype(BF16)
        qnr_ref[0, :, c0:c0 + half] = (rope(qn) * qscale).astype(BF16)
        ga_ref[0, :, c0:c0 + half] = jax.nn.sigmoid(proj(_G_GA + c0, half, wg_ref))
        gb_ref[0, :, c0:c0 + half] = jax.nn.sigmoid(proj(_G_GB + c0, half, wg_ref))
    kcv = proj(_C_KCV, 2 * KV_W)
    kcr_ref[0] = kcv[:, :KV_W]
    vcr_ref[0] = kcv[:, KV_W:]
    ksw_ref[0] = rope(proj(_M_KSW, 2 * KV_W, wm_ref)).astype(BF16)
    vsw_ref[0] = proj(_M_VSW, 2 * KV_W, wm_ref).astype(BF16)
    gns_ref[0] = jax.nn.sigmoid(proj(_M_GNS, LANES, wm_ref))


def _proj_weights(w_in):
    offs = [0]
    for w in (DA_W, DA_W, DA_W, NSA_W) + (KV_W,) * 6 + (GATE_W, D_MODEL, D_MODEL):
        offs.append(offs[-1] + w)
    (o_q, o_k, o_v, o_qn, o_kc, o_vc, o_ks, o_vs, o_kw, o_vw, o_g, o_ga, o_gb, _) = offs

    def paired(o_a, o_b):
        parts = []
        for g in range(NSA_GROUPS):
            parts += [w_in[:, o_a + g * HEAD_DIM:o_a + (g + 1) * HEAD_DIM],
                      w_in[:, o_b + g * HEAD_DIM:o_b + (g + 1) * HEAD_DIM]]
        return jnp.concatenate(parts, axis=1)

    gates = jnp.pad(w_in[:, o_g:o_ga], ((0, 0), (0, LANES - GATE_W)))
    mid = jnp.concatenate([paired(o_ks, o_kw), paired(o_vs, o_vw), gates], axis=1)
    return w_in[:, :o_ks].astype(BF16), mid.astype(BF16), w_in[:, o_ga:].astype(BF16)


def _rope_tables(t):
    pos = jnp.arange(t, dtype=F32)
    inv = 1.0 / (ROPE_THETA ** (jnp.arange(0, HEAD_DIM, 2, dtype=F32) / HEAD_DIM))
    ang = pos[:, None] * inv[None, :]
    cos, sin = jnp.cos(ang), jnp.sin(ang)
    zero = jnp.zeros_like(sin)
    cos128 = jnp.tile(cos, (1, 4))
    sa = jnp.tile(jnp.concatenate([-sin, zero], axis=1), (1, 2))
    sb = jnp.tile(jnp.concatenate([zero, sin], axis=1), (1, 2))
    return cos128, sa, sb


def _inproj(x3, g, w_in):
    b, t, _ = x3.shape
    cos, sa, sb = _rope_tables(t)
    tok = lambda w: pl.BlockSpec((1, TM_PROJ, w), lambda i, j: (i, j, 0))
    tab = pl.BlockSpec((TM_PROJ, LANES), lambda i, j: (j, 0))
    sds = lambda w, dt: jax.ShapeDtypeStruct((b, t, w), dt)
    outs = [(DA_W, BF16)] * 3 + [(NSA_W, BF16)] * 2 + [(KV_W, F32)] * 2 + [(2 * KV_W, BF16)] * 2 \
        + [(LANES, F32), (D_MODEL, F32), (D_MODEL, F32)]
    return pl.pallas_call(
        _proj_kernel,
        grid=(b, t // TM_PROJ),
        in_specs=[tok(D_MODEL), _resident((1, D_MODEL)), _resident((D_MODEL, _HEAD_W)),
                  _resident((D_MODEL, _MID_W)), _resident((D_MODEL, 2 * D_MODEL)), tab, tab, tab],
        out_specs=[tok(w) for w, _ in outs],
        out_shape=[sds(w, dt) for w, dt in outs],
        compiler_params=_params("parallel", "parallel"),
        name="inproj",
    )(x3, g.reshape(1, D_MODEL), *_proj_weights(w_in), cos, sa, sb)


def _fill_value_slab(va_ref, v):
    va_ref[:, :LANES] = v
    va_ref[:, LANES:] = jnp.ones(v.shape, v.dtype)


def _causal_bias(qb):
    tri = lax.broadcasted_iota(jnp.int32, (qb, qb), 1) <= lax.broadcasted_iota(jnp.int32, (qb, qb), 0)
    return jnp.where(tri, 0.0, NEG_INF)


def _causal_probs(s, diag_bias):
    qb = s.shape[1]
    r0 = s.shape[2] - qb
    sd = s[:, :, r0:] + diag_bias[None]
    m = jnp.max(sd, axis=2, keepdims=True)
    if not r0:
        return jnp.exp2(sd - m).astype(BF16)
    sm = s[:, :, :r0]
    m = jnp.maximum(m, jnp.max(sm, axis=2, keepdims=True))
    return jnp.concatenate([jnp.exp2(sm - m).astype(BF16), jnp.exp2(sd - m).astype(BF16)], axis=2)


def _da_kernel(lam_ref, gain_ref, q_ref, k_ref, v_ref, o_ref, va_ref, *, lam_init):
    t = q_ref.shape[1]
    qb = QB_DA
    _fill_value_slab(va_ref, v_ref[0])
    lp = lam_ref[...]
    lam = (jnp.exp(jnp.sum(lp[0:1] * lp[1:2], axis=1, keepdims=True))
           - jnp.exp(jnp.sum(lp[2:3] * lp[3:4], axis=1, keepdims=True)) + lam_init)
    gain = gain_ref[0] * (1.0 - lam_init)
    lane = lax.broadcasted_iota(jnp.int32, (qb, LANES), 1)
    diag_bias = _causal_bias(qb)

    def scores(i):
        r0 = i * qb
        q = q_ref[0, r0:r0 + qb, :]
        zero = jnp.zeros_like(q)
        qs = jnp.concatenate([jnp.where(lane < HEAD_DIM, q, zero), jnp.where(lane >= HEAD_DIM, q, zero)], axis=0)
        return _dot_nt(qs, k_ref[0, :r0 + qb, :]).reshape(2, qb, r0 + qb)

    order = list(reversed(range(t // qb)))
    pending = [scores(i) for i in order[:LOOKAHEAD]]
    for pos, i in enumerate(order):
        r0 = i * qb
        s = pending.pop(0)
        if pos + LOOKAHEAD < len(order):
            pending.append(scores(order[pos + LOOKAHEAD]))
        p = _causal_probs(s, diag_bias)
        oa = _dot(p.reshape(2 * qb, r0 + qb), va_ref[:r0 + qb, :])
        o = oa[:, :LANES] / oa[:, LANES:]
        o = o[:qb] - lam * o[qb:]
        o_ref[0, r0:r0 + qb, :] = (_rms(o, gain)).astype(o_ref.dtype)


def _diff_attention(q, k, v, da_lambda, head_gain, lam_init):
    b, t, _ = q.shape
    head = pl.BlockSpec((1, t, LANES), lambda i, h: (i, 0, h))
    return pl.pallas_call(
        functools.partial(_da_kernel, lam_init=lam_init),
        grid=(b, DA_HEADS),
        in_specs=[pl.BlockSpec((4, HEAD_DIM), lambda i, h: (0, 0)),
                  pl.BlockSpec((1, 1, LANES), lambda i, h: (h, 0, 0)), head, head, head],
        out_specs=head,
        out_shape=jax.ShapeDtypeStruct((b, t, DA_W), BF16),
        scratch_shapes=[pltpu.VMEM((t, 2 * LANES), BF16)],
        compiler_params=_params("parallel", "parallel"),
        name="diffattn",
    )(da_lambda, head_gain.reshape(DA_HEADS, 1, LANES), q, k, v)


def _cmp_kernel(k_ref, v_ref, pos_ref, w1_ref, w2_ref, o_ref):
    assert NSA_GROUPS * HEAD_DIM == LANES
    half = CMP_STRIDE * HEAD_DIM
    n = k_ref.shape[1] // CMP_STRIDE
    lo_lane = lax.broadcasted_iota(jnp.int32, (n, LANES), 1) < HEAD_DIM
    for kv, x_ref in enumerate((k_ref, v_ref)):
        tok = [x_ref[0, pl.ds(l, n, stride=CMP_STRIDE), :] for l in range(CMP_STRIDE)]
        chunks = [[], []]
        for l in range(0, CMP_STRIDE, 2):
            a, b = tok[l], tok[l + 1]
            chunks[0].append(jnp.where(lo_lane, a, pltpu.roll(b, HEAD_DIM, 1)))
            chunks[1].append(jnp.where(lo_lane, pltpu.roll(a, HEAD_DIM, 1), b))
        for g in range(NSA_GROUPS):
            x = jnp.concatenate(chunks[g], axis=1)
            pos = pos_ref[kv]
            xa = (x + pos[:, :half]).astype(BF16)
            xb = (x + pos[:, half:]).astype(BF16)
            a = _dot(xa, w1_ref[kv, :half, :])
            bm = _dot(xb, w1_ref[kv, half:, :])
            hid = a + pltpu.roll(bm, bm.shape[0] - 1, 0)
            act = jax.nn.gelu(hid).astype(BF16)
            o_ref[0, kv * NSA_GROUPS + g] = _dot(act, w2_ref[kv]).astype(o_ref.dtype)


def _compress(kcr, vcr, cmp_pos, cmp_w1, cmp_w2):
    b, t, _ = kcr.shape
    n = t // CMP_STRIDE
    w2d = jnp.concatenate([cmp_w2, cmp_w2], axis=-1).astype(BF16)
    raw = pl.BlockSpec((1, t, KV_W), lambda i: (i, 0, 0))
    return pl.pallas_call(
        _cmp_kernel,
        grid=(b,),
        in_specs=[raw, raw,
                  pl.BlockSpec((2, 1, CMP_BLOCK * HEAD_DIM), lambda i: (0, 0, 0)),
                  pl.BlockSpec((2, CMP_BLOCK * HEAD_DIM, CMP_HIDDEN), lambda i: (0, 0, 0)),
                  pl.BlockSpec((2, CMP_HIDDEN, LANES), lambda i: (0, 0, 0))],
        out_specs=pl.BlockSpec((1, 2 * NSA_GROUPS, n, LANES), lambda i: (i, 0, 0, 0)),
        out_shape=jax.ShapeDtypeStruct((b, 2 * NSA_GROUPS, n, LANES), BF16),
        compiler_params=_params("parallel"),
        name="compress",
    )(kcr, vcr, cmp_pos.reshape(2, 1, CMP_BLOCK * HEAD_DIM), cmp_w1.astype(BF16), w2d)


def _split3(x):
    hi = x.astype(BF16)
    r1 = x - hi.astype(F32)
    mid = r1.astype(BF16)
    lo = (r1 - mid.astype(F32)).astype(BF16)
    return hi, mid, lo


def _stack_heads(qblk, heads, lo_lane):
    zero = jnp.zeros((qblk.shape[0], LANES), qblk.dtype)
    parts = []
    for j in heads:
        pair = qblk[:, (j // 2) * LANES:(j // 2 + 1) * LANES]
        parts.append(jnp.where(lo_lane if j % 2 == 0 else ~lo_lane, pair, zero))
    return jnp.concatenate(parts, axis=0)


def _sel_kernel(qc_ref, kc_ref, vc_ref, sel_ref, ocmp_ref):
    t = qc_ref.shape[1]
    qb, hp = QB_NSA, NSA_HPG
    n_slc = t // SLC_BLOCK
    lane = lax.broadcasted_iota(jnp.int32, (qb, LANES), 1)
    lo_lane = lane < HEAD_DIM
    bias_lane = (lane & (HEAD_DIM - 1)) < n_slc
    ov_s = lax.broadcasted_iota(jnp.int32, (LANES, LANES), 0)
    ov_n = lax.broadcasted_iota(jnp.int32, (LANES, LANES), 1)
    ov_t = jnp.where(ov_s < n_slc,
                     jnp.where(ov_n * CMP_STRIDE < (ov_s + 1) * SLC_BLOCK,
                               jnp.where(ov_n * CMP_STRIDE + CMP_BLOCK > ov_s * SLC_BLOCK, 1.0, 0.0), 0.0),
                     0.0).astype(BF16)
    s_idx = lax.broadcasted_iota(jnp.int32, (n_slc, qb), 0)
    t_off = lax.broadcasted_iota(jnp.int32, (n_slc, qb), 1)
    pad = jnp.zeros((HEAD_DIM - n_slc, qb), F32)
    vc_t = vc_ref[0, 0].astype(F32).T[:HEAD_DIM].astype(BF16)

    for i in range(t // qb):
        r0 = i * qb
        rows = slice(r0, r0 + qb)
        n_any = min(LANES, (r0 + qb) // CMP_STRIDE)
        n_all = (max(r0 - (CMP_BLOCK - 1) + CMP_STRIDE, 0) // CMP_STRIDE) // 8 * 8
        q_c = _stack_heads(qc_ref[0, rows, :], range(hp), lo_lane)
        s = _dot_nt(kc_ref[0, 0, :n_any, :], q_c)
        blk = n_all + lax.broadcasted_iota(jnp.int32, (n_any - n_all, hp * qb), 0)
        tq = r0 + (lax.broadcasted_iota(jnp.int32, (n_any - n_all, hp * qb), 1) & (qb - 1))
        valid = (blk * CMP_STRIDE + (CMP_BLOCK - 1)) <= tq
        edge = s[n_all:] + jnp.where(valid, 0.0, NEG_INF)
        m = jnp.max(edge, axis=0, keepdims=True)
        if n_all:
            m = jnp.maximum(m, jnp.max(s[:n_all], axis=0, keepdims=True))
        e_edge = jnp.exp2(edge - m)
        l = jnp.sum(e_edge, axis=0, keepdims=True)
        if n_all:
            e_all = jnp.exp2(s[:n_all] - m)
            l = l + jnp.sum(e_all, axis=0, keepdims=True)
        inv = 1.0 / l
        p = jnp.where(valid, e_edge * inv, 0.0)
        if n_all:
            p = jnp.concatenate([e_all * inv, p], axis=0)
        pb = p.astype(BF16)
        if n_any < LANES:
            pb = jnp.concatenate([pb, jnp.zeros((LANES - n_any, hp * qb), BF16)], axis=0)
        o_t = _dot(vc_t, pb)
        for pr in range(hp // 2):
            both = jnp.concatenate([o_t[:, (2 * pr) * qb:(2 * pr + 1) * qb],
                                    o_t[:, (2 * pr + 1) * qb:(2 * pr + 2) * qb]], axis=0)
            ocmp_ref[0, rows, pr * LANES:(pr + 1) * LANES] = both.T
        psum = sum(p[:, j * qb:(j + 1) * qb] for j in range(hp))
        if n_any < LANES:
            psum = jnp.concatenate([psum, jnp.zeros((LANES - n_any, qb), F32)], axis=0)
        imp_t = sum(_dot(ov_t, term) for term in _split3(psum))
        blk_t = lax.shift_right_logical(r0 + t_off, SLC_SHIFT)
        causal = s_idx <= blk_t
        n_causal = (r0 + qb) // SLC_BLOCK
        if n_causal <= SLC_TOPN:
            sel_t = jnp.where(causal, 1.0, 0.0)
        else:
            forced = (s_idx == 0) | (s_idx == blk_t) | (s_idx == blk_t - 1)
            score = jnp.where(forced, FORCE_SCORE, jnp.where(causal, imp_t[:n_slc], -1.0))
            rank = jnp.zeros((n_slc, qb), F32)
            for sp in range(n_causal):
                other = score[sp:sp + 1, :]
                tie = jnp.where(s_idx > sp, 1.0, 0.0)
                rank = rank + jnp.where(other > score, 1.0, jnp.where(other == score, tie, 0.0))
            sel_t = jnp.where(causal, jnp.where(rank < SLC_TOPN, 1.0, 0.0), 0.0)
        both = jnp.concatenate([sel_t, pad, sel_t, pad], axis=0).T
        sel_ref[0, 0, rows, :] = jnp.where(bias_lane, jnp.where(both > 0.5, 0.0, NEG_INF), 0.0).astype(BF16)


def _select(qc, kvc):
    b, t, _ = qc.shape
    n = t // CMP_STRIDE
    grp = pl.BlockSpec((1, t, GROUP_W), lambda i, g: (i, 0, g))
    return pl.pallas_call(
        _sel_kernel,
        grid=(b, NSA_GROUPS),
        in_specs=[grp, pl.BlockSpec((1, 1, n, LANES), lambda i, g: (i, g, 0, 0)),
                  pl.BlockSpec((1, 1, n, LANES), lambda i, g: (i, NSA_GROUPS + g, 0, 0))],
        out_specs=[pl.BlockSpec((1, 1, t, LANES), lambda i, g: (i, g, 0, 0)), grp],
        out_shape=[jax.ShapeDtypeStruct((b, NSA_GROUPS, t, LANES), BF16),
                   jax.ShapeDtypeStruct((b, t, NSA_W), F32)],
        compiler_params=_params("parallel", "parallel"),
        name="nsa_sel",
    )(qc, kvc, kvc)


def _swap_halves(x):
    return pltpu.roll(x.astype(F32), HEAD_DIM, 1).astype(x.dtype)


def _nsa_kernel(qr_ref, ocmp_ref, ksw_ref, vsw_ref, gate_ref, sel_ref, o_ref, vsa_ref, vwa_ref, kse_ref):
    t = qr_ref.shape[1]
    qb = QB_NSA
    group = pl.program_id(1)

    lane_t = lax.broadcasted_iota(jnp.int32, (t, LANES), 1)
    lo_t = lane_t < HEAD_DIM
    vsw = vsw_ref[0]
    vws = _swap_halves(vsw)
    _fill_value_slab(vsa_ref, jnp.where(lo_t, vsw, vws))
    _fill_value_slab(vwa_ref, jnp.where(lo_t, vws, vsw))
    key_blk = lax.shift_right_logical(lax.broadcasted_iota(jnp.int32, (t, LANES), 0), SLC_SHIFT)
    onehot = jnp.where((lane_t & (HEAD_DIM - 1)) == key_blk, 1.0, 0.0).astype(BF16)
    kse_ref[...] = jnp.where(lo_t, ksw_ref[0], onehot)

    lane = lax.broadcasted_iota(jnp.int32, (qb, LANES), 1)
    lo_lane = lane < HEAD_DIM
    diag_bias = _causal_bias(qb)

    def win_bias(w0, r0):
        n = r0 + qb - w0
        kpos = w0 + lax.broadcasted_iota(jnp.int32, (qb, n), 1)
        qpos = r0 + lax.broadcasted_iota(jnp.int32, (qb, n), 0)
        return jnp.where(kpos <= qpos, jnp.where(kpos > qpos - WIN, 0.0, NEG_INF), NEG_INF)

    steady_win_bias = win_bias(0, WIN)

    def normalised(oa):
        return oa[:, :LANES] / oa[:, LANES:]

    def scores(i, pp):
        r0 = i * qb
        rows = slice(r0, r0 + qb)
        qr = qr_ref[0, rows, pp * LANES:(pp + 1) * LANES]
        zero = jnp.zeros_like(qr)
        selb = sel_ref[0, 0, rows, :]
        q_sw = _swap_halves(qr)
        qs = jnp.concatenate([jnp.where(lo_lane, qr, selb), jnp.where(lo_lane, q_sw, selb)], axis=0)
        s_slc = _dot_nt(qs, kse_ref[:r0 + qb, :]).reshape(2, qb, r0 + qb)
        w0 = max(r0 - WIN, 0)
        q2 = jnp.concatenate([jnp.where(lo_lane, zero, q_sw), jnp.where(lo_lane, zero, qr)], axis=0)
        s_win = _dot_nt(q2, ksw_ref[0, w0:r0 + qb, :]).reshape(2, qb, r0 + qb - w0)
        return s_slc, s_win

    order = [(i, pp) for i in reversed(range(t // qb)) for pp in range(PAIRS_PER_STEP)]
    pending = [scores(*o) for o in order[:LOOKAHEAD]]
    for pos, (i, pp) in enumerate(order):
        pair = pl.program_id(2) * PAIRS_PER_STEP + pp
        cols = slice(pp * LANES, (pp + 1) * LANES)
        r0 = i * qb
        rows = slice(r0, r0 + qb)
        s_slc, s_win = pending.pop(0)
        if pos + LOOKAHEAD < len(order):
            pending.append(scores(*order[pos + LOOKAHEAD]))
        p = _causal_probs(s_slc, diag_bias)
        o_slc = normalised(_dot(p.reshape(2 * qb, r0 + qb), vsa_ref[:r0 + qb, :]))
        w0 = max(r0 - WIN, 0)
        sw = s_win + (steady_win_bias if r0 >= WIN else win_bias(w0, r0))[None]
        pw = jnp.exp2(sw - jnp.max(sw, axis=2, keepdims=True)).astype(BF16).reshape(2 * qb, r0 + qb - w0)
        o_win = normalised(_dot(pw, vwa_ref[w0:r0 + qb, :]))

        gate = gate_ref[0, rows, :]
        o_cmp = ocmp_ref[0, rows, cols]
        outs = []
        for par in range(2):
            col = GATE_PER_GROUP * group + 6 * pair + 3 * par
            g = [jnp.sum(jnp.where(lane == col + br, gate, 0.0), axis=1, keepdims=True) for br in range(3)]
            hr = slice(par * qb, (par + 1) * qb)
            outs.append(g[0] * o_cmp + g[1] * o_slc[hr] + g[2] * o_win[hr])
        o_ref[0, rows, cols] = jnp.where(lo_lane, outs[0], outs[1]).astype(o_ref.dtype)


def _nsa(qr, ocmp, ksw, vsw, gns, selb):
    b, t, _ = qr.shape
    steps = PAIRS // PAIRS_PER_STEP
    pair = pl.BlockSpec((1, t, PAIRS_PER_STEP * LANES), lambda i, g, p: (i, 0, g * steps + p))
    grp = pl.BlockSpec((1, t, LANES), lambda i, g, p: (i, 0, g))
    gates = pl.BlockSpec((1, t, LANES), lambda i, g, p: (i, 0, 0))
    sel = pl.BlockSpec((1, 1, t, LANES), lambda i, g, p: (i, g, 0, 0))
    slab = pltpu.VMEM((t, 2 * LANES), BF16)
    keys = pltpu.VMEM((t, LANES), BF16)
    return pl.pallas_call(
        _nsa_kernel,
        grid=(b, NSA_GROUPS, steps),
        in_specs=[pair, pair, grp, grp, gates, sel],
        out_specs=pair,
        out_shape=jax.ShapeDtypeStruct((b, t, NSA_W), BF16),
        scratch_shapes=[slab, slab, keys],
        compiler_params=_params("parallel", "parallel", "parallel"),
        name="nsa_attn",
    )(qr, ocmp, ksw, vsw, gns, selb)


def _merge_ffn_kernel(x_ref, yda_ref, yns_ref, ga_ref, gb_ref, wa_ref, wb_ref, wo_ref,
                      g_ref, w1_ref, w3_ref, w2_ref, fg_ref, o_ref):
    merged = ga_ref[...] * _dot(yda_ref[...], wa_ref[...]) + gb_ref[...] * _dot(yns_ref[...], wb_ref[...])
    x = x_ref[...] + _dot(merged.astype(BF16), wo_ref[...])
    o_ref[...] = _rms(_swiglu_residual(x, g_ref[...], w1_ref, w3_ref, w2_ref), fg_ref[...])


def _merge_ffn(x2, yda, yns, ga, gb, w_da, w_nsa, w_out, g, w1, w3, w2, final_g):
    n = x2.shape[0]
    row = pl.BlockSpec((TM_FFN, D_MODEL), lambda i: (i, 0))
    wspec = _resident((D_MODEL, D_MODEL))
    return pl.pallas_call(
        _merge_ffn_kernel,
        grid=(n // TM_FFN,),
        in_specs=[row] * 5 + [wspec] * 3 + _ffn_specs() + [_resident((1, D_MODEL))],
        out_specs=row,
        out_shape=jax.ShapeDtypeStruct((n, D_MODEL), F32),
        compiler_params=_params("parallel"),
        name="merge_ffn",
    )(x2, yda, yns, ga, gb, w_da.astype(BF16), w_nsa.astype(BF16), w_out.astype(BF16),
      g.reshape(1, D_MODEL), w1.astype(BF16), w3.astype(BF16), w2.astype(BF16), final_g.reshape(1, D_MODEL))


def kernel(x, ffn1_norm, ffn1_w1, ffn1_w3, ffn1_w2, mix_norm, w_in, da_lambda, da_head_norm, cmp_pos, cmp_w1,
           cmp_w2, w_proj_da, w_proj_nsa, w_out, ffn2_norm, ffn2_w1, ffn2_w3, ffn2_w2, final_norm):
    b, t, d = x.shape
    depth = ffn1_norm.shape[0]
    assert depth == 1, "one decoder layer per call"
    lam_init = 0.8 - 0.6 * math.exp(-0.3 * 0)
    x2 = _ffn(x.reshape(b * t, d), ffn1_norm[0], ffn1_w1[0], ffn1_w3[0], ffn1_w2[0])
    (qda, kda, vda, qnr, qnc, kcr, vcr, ksw, vsw, gns, ga, gb) = _inproj(
        x2.reshape(b, t, d), mix_norm[0], w_in[0])
    yda = _diff_attention(qda, kda, vda, da_lambda[0], da_head_norm[0], lam_init)
    kvc = _compress(kcr, vcr, cmp_pos[0], cmp_w1[0], cmp_w2[0])
    selb, ocmp = _select(qnc, kvc)
    yns = _nsa(qnr, ocmp, ksw, vsw, gns, selb)
    flat = lambda a: a.reshape(b * t, a.shape[-1])
    out = _merge_ffn(x2, flat(yda), flat(yns), flat(ga), flat(gb), w_proj_da[0], w_proj_nsa[0], w_out[0],
                     ffn2_norm[0], ffn2_w1[0], ffn2_w3[0], ffn2_w2[0], final_norm)
    return out.reshape(b, t, d)
```

```python
import functools
import math

import jax
import jax.numpy as jnp
from jax import lax
from jax.experimental import pallas as pl
from jax.experimental.pallas import tpu as pltpu

D_MODEL = 1024
HEAD_DIM = 64
DA_HEADS = 8
NSA_HEADS = 16
NSA_GROUPS = 2
NSA_HPG = NSA_HEADS // NSA_GROUPS
CMP_BLOCK = 32
CMP_STRIDE = 16
CMP_HIDDEN = 256
SLC_BLOCK = 64
SLC_TOPN = 16
WIN = 512
D_FF = 2816
ROPE_THETA = 10000.0
EPS = 1e-6
NEG_INF = -1e30
FORCE_SCORE = 1e9
LOG2E = math.log2(math.e)

LANES = 128
DA_W = DA_HEADS * 2 * HEAD_DIM
NSA_W = NSA_HEADS * HEAD_DIM
GROUP_W = NSA_HPG * HEAD_DIM
PAIRS = NSA_HPG // 2
KV_W = NSA_GROUPS * HEAD_DIM
GATE_W = 3 * NSA_HEADS
GATE_PER_GROUP = 3 * NSA_HPG
SLC_SHIFT = SLC_BLOCK.bit_length() - 1
assert 1 << SLC_SHIFT == SLC_BLOCK

TM_FFN = 512
TM_PROJ = 256
FF_CHUNK = 256
QB_DA = 256
QB_NSA = 256
PAIRS_PER_STEP = 1
LOOKAHEAD = 2
assert QB_NSA & (QB_NSA - 1) == 0
VMEM_LIMIT = 56 * 1024 * 1024

BF16 = jnp.bfloat16
F32 = jnp.float32


def _rms(x, g):
    return x * lax.rsqrt(jnp.mean(x * x, axis=-1, keepdims=True) + EPS) * g


def _dot(a, b):
    return jnp.dot(a, b, preferred_element_type=F32)


def _dot_nt(a, b):
    return lax.dot_general(a, b, (((1,), (1,)), ((), ())), preferred_element_type=F32)


def _resident(shape):
    nd = len(shape)
    return pl.BlockSpec(shape, lambda *_: (0,) * nd, pipeline_mode=pl.Buffered(1))


def _params(*sem):
    return pltpu.CompilerParams(dimension_semantics=sem, vmem_limit_bytes=VMEM_LIMIT)


def _swiglu_residual(x, g, w1_ref, w3_ref, w2_ref):
    h = _rms(x, g).astype(BF16)
    acc = jnp.zeros_like(x)
    for c in range(D_FF // FF_CHUNK):
        sl = slice(c * FF_CHUNK, (c + 1) * FF_CHUNK)
        a = _dot(h, w1_ref[:, sl])
        b = _dot(h, w3_ref[:, sl])
        u = (a * jax.nn.sigmoid(a) * b).astype(BF16)
        acc = acc + _dot(u, w2_ref[sl, :])
    return x + 0.5 * acc


def _ffn_specs():
    return [_resident((1, D_MODEL)), _resident((D_MODEL, D_FF)), _resident((D_MODEL, D_FF)),
            _resident((D_FF, D_MODEL))]


_C_QDA, _C_KDA, _C_VDA, _C_QNS = 0, DA_W, 2 * DA_W, 3 * DA_W
_C_KCV = 3 * DA_W + NSA_W
_HEAD_W = _C_KCV + 2 * KV_W
_M_KSW = 0
_M_VSW = 2 * KV_W
_M_GNS = 4 * KV_W
_MID_W = 4 * KV_W + LANES
_G_GA, _G_GB = 0, D_MODEL


def _ffn_proj_kernel(x_ref, fg_ref, w1_ref, w3_ref, w2_ref, g_ref, wh_ref, wm_ref, wg_ref, cos_ref, sa_ref, sb_ref,
                     x1_ref, *out_refs):
    x1 = _swiglu_residual(x_ref[0], fg_ref[...], w1_ref, w3_ref, w2_ref)
    x1_ref[0] = x1
    _project(x1, g_ref, wh_ref, wm_ref, wg_ref, cos_ref, sa_ref, sb_ref, *out_refs)


def _project(x, g_ref, wh_ref, wm_ref, wg_ref, cos_ref, sa_ref, sb_ref,
             qda_ref, kda_ref, vda_ref, qnr_ref, qnc_ref, kcr_ref, vcr_ref,
             ksw_ref, vsw_ref, gns_ref, ga_ref, gb_ref):
    h = _rms(x, g_ref[...]).astype(BF16)
    cos, sa, sb = cos_ref[...], sa_ref[...], sb_ref[...]

    def proj(c0, width, w_ref=wh_ref):
        return _dot(h, w_ref[:, c0:c0 + width])

    def rope(y):
        cols = []
        for c in range(y.shape[1] // LANES):
            v = y[:, c * LANES:(c + 1) * LANES]
            cols.append(v * cos + pltpu.roll(v, LANES - 32, 1) * sa + pltpu.roll(v, 32, 1) * sb)
        return cols[0] if len(cols) == 1 else jnp.concatenate(cols, axis=1)

    qscale = HEAD_DIM ** -0.5 * LOG2E
    half = DA_W // 2
    for c0 in (0, half):
        qda_ref[0, :, c0:c0 + half] = (rope(proj(_C_QDA + c0, half)) * qscale).astype(BF16)
        kda_ref[0, :, c0:c0 + half] = rope(proj(_C_KDA + c0, half)).astype(BF16)
        vda_ref[0, :, c0:c0 + half] = proj(_C_VDA + c0, half).astype(BF16)
        qn = proj(_C_QNS + c0, half)
        qnc_ref[0, :, c0:c0 + half] = (qn * qscale).astype(BF16)
        qnr_ref[0, :, c0:c0 + half] = (rope(qn) * qscale).astype(BF16)
        ga_ref[0, :, c0:c0 + half] = jax.nn.sigmoid(proj(_G_GA + c0, half, wg_ref))
        gb_ref[0, :, c0:c0 + half] = jax.nn.sigmoid(proj(_G_GB + c0, half, wg_ref))
    kcv = proj(_C_KCV, 2 * KV_W)
    kcr_ref[0] = kcv[:, :KV_W]
    vcr_ref[0] = kcv[:, KV_W:]
    ksw_ref[0] = rope(proj(_M_KSW, 2 * KV_W, wm_ref)).astype(BF16)
    vsw_ref[0] = proj(_M_VSW, 2 * KV_W, wm_ref).astype(BF16)
    gns_ref[0] = jax.nn.sigmoid(proj(_M_GNS, LANES, wm_ref))


def _proj_weights(w_in):
    offs = [0]
    for w in (DA_W, DA_W, DA_W, NSA_W) + (KV_W,) * 6 + (GATE_W, D_MODEL, D_MODEL):
        offs.append(offs[-1] + w)
    (o_q, o_k, o_v, o_qn, o_kc, o_vc, o_ks, o_vs, o_kw, o_vw, o_g, o_ga, o_gb, _) = offs

    def paired(o_a, o_b):
        parts = []
        for g in range(NSA_GROUPS):
            parts += [w_in[:, o_a + g * HEAD_DIM:o_a + (g + 1) * HEAD_DIM],
                      w_in[:, o_b + g * HEAD_DIM:o_b + (g + 1) * HEAD_DIM]]
        return jnp.concatenate(parts, axis=1)

    gates = jnp.pad(w_in[:, o_g:o_ga], ((0, 0), (0, LANES - GATE_W)))
    mid = jnp.concatenate([paired(o_ks, o_kw), paired(o_vs, o_vw), gates], axis=1)
    return w_in[:, :o_ks].astype(BF16), mid.astype(BF16), w_in[:, o_ga:].astype(BF16)


def _rope_tables(t):
    pos = jnp.arange(t, dtype=F32)
    inv = 1.0 / (ROPE_THETA ** (jnp.arange(0, HEAD_DIM, 2, dtype=F32) / HEAD_DIM))
    ang = pos[:, None] * inv[None, :]
    cos, sin = jnp.cos(ang), jnp.sin(ang)
    zero = jnp.zeros_like(sin)
    cos128 = jnp.tile(cos, (1, 4))
    sa = jnp.tile(jnp.concatenate([-sin, zero], axis=1), (1, 2))
    sb = jnp.tile(jnp.concatenate([zero, sin], axis=1), (1, 2))
    return cos128, sa, sb


def _ffn_inproj(x3, ffn_g, w1, w3, w2, g, w_in):
    b, t, _ = x3.shape
    cos, sa, sb = _rope_tables(t)
    tok = lambda w: pl.BlockSpec((1, TM_PROJ, w), lambda i, j: (i, j, 0))
    tab = pl.BlockSpec((TM_PROJ, LANES), lambda i, j: (j, 0))
    sds = lambda w, dt: jax.ShapeDtypeStruct((b, t, w), dt)
    outs = [(D_MODEL, F32)] + [(DA_W, BF16)] * 3 + [(NSA_W, BF16)] * 2 + [(KV_W, F32)] * 2 \
        + [(2 * KV_W, BF16)] * 2 + [(LANES, F32), (D_MODEL, F32), (D_MODEL, F32)]
    return pl.pallas_call(
        _ffn_proj_kernel,
        grid=(b, t // TM_PROJ),
        in_specs=[tok(D_MODEL)] + _ffn_specs() + [_resident((1, D_MODEL)), _resident((D_MODEL, _HEAD_W)),
                                                 _resident((D_MODEL, _MID_W)), _resident((D_MODEL, 2 * D_MODEL)),
                                                 tab, tab, tab],
        out_specs=[tok(w) for w, _ in outs],
        out_shape=[sds(w, dt) for w, dt in outs],
        compiler_params=_params("parallel", "parallel"),
        name="ffn_inproj",
    )(x3, ffn_g.reshape(1, D_MODEL), w1.astype(BF16), w3.astype(BF16), w2.astype(BF16),
      g.reshape(1, D_MODEL), *_proj_weights(w_in), cos, sa, sb)


def _fill_value_slab(va_ref, v):
    va_ref[:, :LANES] = v
    va_ref[:, LANES:] = jnp.ones(v.shape, v.dtype)


def _causal_bias(qb):
    tri = lax.broadcasted_iota(jnp.int32, (qb, qb), 1) <= lax.broadcasted_iota(jnp.int32, (qb, qb), 0)
    return jnp.where(tri, 0.0, NEG_INF)


def _causal_probs(s, diag_bias):
    qb = s.shape[1]
    r0 = s.shape[2] - qb
    sd = s[:, :, r0:] + diag_bias[None]
    m = jnp.max(sd, axis=2, keepdims=True)
    if not r0:
        return jnp.exp2(sd - m).astype(BF16)
    sm = s[:, :, :r0]
    m = jnp.maximum(m, jnp.max(sm, axis=2, keepdims=True))
    return jnp.concatenate([jnp.exp2(sm - m).astype(BF16), jnp.exp2(sd - m).astype(BF16)], axis=2)


def _da_kernel(lam_ref, gain_ref, q_ref, k_ref, v_ref, o_ref, va_ref, *, lam_init):
    t = q_ref.shape[1]
    qb = QB_DA
    _fill_value_slab(va_ref, v_ref[0])
    lp = lam_ref[...]
    lam = (jnp.exp(jnp.sum(lp[0:1] * lp[1:2], axis=1, keepdims=True))
           - jnp.exp(jnp.sum(lp[2:3] * lp[3:4], axis=1, keepdims=True)) + lam_init)
    gain = gain_ref[0] * (1.0 - lam_init)
    lane = lax.broadcasted_iota(jnp.int32, (qb, LANES), 1)
    diag_bias = _causal_bias(qb)

    def scores(i):
        r0 = i * qb
        q = q_ref[0, r0:r0 + qb, :]
        zero = jnp.zeros_like(q)
        qs = jnp.concatenate([jnp.where(lane < HEAD_DIM, q, zero), jnp.where(lane >= HEAD_DIM, q, zero)], axis=0)
        return _dot_nt(qs, k_ref[0, :r0 + qb, :]).reshape(2, qb, r0 + qb)

    order = list(reversed(range(t // qb)))
    pending = [scores(i) for i in order[:LOOKAHEAD]]
    for pos, i in enumerate(order):
        r0 = i * qb
        s = pending.pop(0)
        if pos + LOOKAHEAD < len(order):
            pending.append(scores(order[pos + LOOKAHEAD]))
        p = _causal_probs(s, diag_bias)
        oa = _dot(p.reshape(2 * qb, r0 + qb), va_ref[:r0 + qb, :])
        o = oa[:, :LANES] / oa[:, LANES:]
        o = o[:qb] - lam * o[qb:]
        o_ref[0, r0:r0 + qb, :] = (_rms(o, gain)).astype(o_ref.dtype)


def _diff_attention(q, k, v, da_lambda, head_gain, lam_init):
    b, t, _ = q.shape
    head = pl.BlockSpec((1, t, LANES), lambda i, h: (i, 0, h))
    return pl.pallas_call(
        functools.partial(_da_kernel, lam_init=lam_init),
        grid=(b, DA_HEADS),
        in_specs=[pl.BlockSpec((4, HEAD_DIM), lambda i, h: (0, 0)),
                  pl.BlockSpec((1, 1, LANES), lambda i, h: (h, 0, 0)), head, head, head],
        out_specs=head,
        out_shape=jax.ShapeDtypeStruct((b, t, DA_W), BF16),
        scratch_shapes=[pltpu.VMEM((t, 2 * LANES), BF16)],
        compiler_params=_params("parallel", "parallel"),
        name="diffattn",
    )(da_lambda, head_gain.reshape(DA_HEADS, 1, LANES), q, k, v)


def _cmp_kernel(k_ref, v_ref, pos_ref, w1_ref, w2_ref, o_ref):
    assert NSA_GROUPS * HEAD_DIM == LANES
    half = CMP_STRIDE * HEAD_DIM
    n = k_ref.shape[1] // CMP_STRIDE
    lo_lane = lax.broadcasted_iota(jnp.int32, (n, LANES), 1) < HEAD_DIM
    for kv, x_ref in enumerate((k_ref, v_ref)):
        tok = [x_ref[0, pl.ds(l, n, stride=CMP_STRIDE), :] for l in range(CMP_STRIDE)]
        chunks = [[], []]
        for l in range(0, CMP_STRIDE, 2):
            a, b = tok[l], tok[l + 1]
            chunks[0].append(jnp.where(lo_lane, a, pltpu.roll(b, HEAD_DIM, 1)))
            chunks[1].append(jnp.where(lo_lane, pltpu.roll(a, HEAD_DIM, 1), b))
        for g in range(NSA_GROUPS):
            x = jnp.concatenate(chunks[g], axis=1)
            pos = pos_ref[kv]
            xa = (x + pos[:, :half]).astype(BF16)
            xb = (x + pos[:, half:]).astype(BF16)
            a = _dot(xa, w1_ref[kv, :half, :])
            bm = _dot(xb, w1_ref[kv, half:, :])
            hid = a + pltpu.roll(bm, bm.shape[0] - 1, 0)
            act = jax.nn.gelu(hid).astype(BF16)
            o_ref[0, kv * NSA_GROUPS + g] = _dot(act, w2_ref[kv]).astype(o_ref.dtype)


def _compress(kcr, vcr, cmp_pos, cmp_w1, cmp_w2):
    b, t, _ = kcr.shape
    n = t // CMP_STRIDE
    w2d = jnp.concatenate([cmp_w2, cmp_w2], axis=-1).astype(BF16)
    raw = pl.BlockSpec((1, t, KV_W), lambda i: (i, 0, 0))
    return pl.pallas_call(
        _cmp_kernel,
        grid=(b,),
        in_specs=[raw, raw,
                  pl.BlockSpec((2, 1, CMP_BLOCK * HEAD_DIM), lambda i: (0, 0, 0)),
                  pl.BlockSpec((2, CMP_BLOCK * HEAD_DIM, CMP_HIDDEN), lambda i: (0, 0, 0)),
                  pl.BlockSpec((2, CMP_HIDDEN, LANES), lambda i: (0, 0, 0))],
        out_specs=pl.BlockSpec((1, 2 * NSA_GROUPS, n, LANES), lambda i: (i, 0, 0, 0)),
        out_shape=jax.ShapeDtypeStruct((b, 2 * NSA_GROUPS, n, LANES), BF16),
        compiler_params=_params("parallel"),
        name="compress",
    )(kcr, vcr, cmp_pos.reshape(2, 1, CMP_BLOCK * HEAD_DIM), cmp_w1.astype(BF16), w2d)


def _split3(x):
    hi = x.astype(BF16)
    r1 = x - hi.astype(F32)
    mid = r1.astype(BF16)
    lo = (r1 - mid.astype(F32)).astype(BF16)
    return hi, mid, lo


def _stack_heads(qblk, heads, lo_lane):
    zero = jnp.zeros((qblk.shape[0], LANES), qblk.dtype)
    parts = []
    for j in heads:
        pair = qblk[:, (j // 2) * LANES:(j // 2 + 1) * LANES]
        parts.append(jnp.where(lo_lane if j % 2 == 0 else ~lo_lane, pair, zero))
    return jnp.concatenate(parts, axis=0)


def _sel_kernel(qc_ref, kc_ref, vc_ref, sel_ref, ocmp_ref):
    t = qc_ref.shape[1]
    qb, hp = QB_NSA, NSA_HPG
    n_slc = t // SLC_BLOCK
    lane = lax.broadcasted_iota(jnp.int32, (qb, LANES), 1)
    lo_lane = lane < HEAD_DIM
    bias_lane = (lane & (HEAD_DIM - 1)) < n_slc
    ov_s = lax.broadcasted_iota(jnp.int32, (LANES, LANES), 0)
    ov_n = lax.broadcasted_iota(jnp.int32, (LANES, LANES), 1)
    ov_t = jnp.where(ov_s < n_slc,
                     jnp.where(ov_n * CMP_STRIDE < (ov_s + 1) * SLC_BLOCK,
                               jnp.where(ov_n * CMP_STRIDE + CMP_BLOCK > ov_s * SLC_BLOCK, 1.0, 0.0), 0.0),
                     0.0).astype(BF16)
    s_idx = lax.broadcasted_iota(jnp.int32, (n_slc, qb), 0)
    t_off = lax.broadcasted_iota(jnp.int32, (n_slc, qb), 1)
    pad = jnp.zeros((HEAD_DIM - n_slc, qb), F32)
    vc_t = vc_ref[0, 0].astype(F32).T[:HEAD_DIM].astype(BF16)

    for i in range(t // qb):
        r0 = i * qb
        rows = slice(r0, r0 + qb)
        n_any = min(LANES, (r0 + qb) // CMP_STRIDE)
        n_all = (max(r0 - (CMP_BLOCK - 1) + CMP_STRIDE, 0) // CMP_STRIDE) // 8 * 8
        q_c = _stack_heads(qc_ref[0, rows, :], range(hp), lo_lane)
        s = _dot_nt(kc_ref[0, 0, :n_any, :], q_c)
        blk = n_all + lax.broadcasted_iota(jnp.int32, (n_any - n_all, hp * qb), 0)
        tq = r0 + (lax.broadcasted_iota(jnp.int32, (n_any - n_all, hp * qb), 1) & (qb - 1))
        valid = (blk * CMP_STRIDE + (CMP_BLOCK - 1)) <= tq
        edge = s[n_all:] + jnp.where(valid, 0.0, NEG_INF)
        m = jnp.max(edge, axis=0, keepdims=True)
        if n_all:
            m = jnp.maximum(m, jnp.max(s[:n_all], axis=0, keepdims=True))
        e_edge = jnp.exp2(edge - m)
        l = jnp.sum(e_edge, axis=0, keepdims=True)
        if n_all:
            e_all = jnp.exp2(s[:n_all] - m)
            l = l + jnp.sum(e_all, axis=0, keepdims=True)
        inv = 1.0 / l
        p = jnp.where(valid, e_edge * inv, 0.0)
        if n_all:
            p = jnp.concatenate([e_all * inv, p], axis=0)
        pb = p.astype(BF16)
        if n_any < LANES:
            pb = jnp.concatenate([pb, jnp.zeros((LANES - n_any, hp * qb), BF16)], axis=0)
        o_t = _dot(vc_t, pb)
        for pr in range(hp // 2):
            both = jnp.concatenate([o_t[:, (2 * pr) * qb:(2 * pr + 1) * qb],
                                    o_t[:, (2 * pr + 1) * qb:(2 * pr + 2) * qb]], axis=0)
            ocmp_ref[0, rows, pr * LANES:(pr + 1) * LANES] = both.T
        psum = sum(p[:, j * qb:(j + 1) * qb] for j in range(hp))
        if n_any < LANES:
            psum = jnp.concatenate([psum, jnp.zeros((LANES - n_any, qb), F32)], axis=0)
        imp_t = sum(_dot(ov_t, term) for term in _split3(psum))
        blk_t = lax.shift_right_logical(r0 + t_off, SLC_SHIFT)
        causal = s_idx <= blk_t
        n_causal = (r0 + qb) // SLC_BLOCK
        if n_causal <= SLC_TOPN:
            sel_t = jnp.where(causal, 1.0, 0.0)
        else:
            forced = (s_idx == 0) | (s_idx == blk_t) | (s_idx == blk_t - 1)
            score = jnp.where(forced, FORCE_SCORE, jnp.where(causal, imp_t[:n_slc], -1.0))
            rank = jnp.zeros((n_slc, qb), F32)
            for sp in range(n_causal):
                other = score[sp:sp + 1, :]
                tie = jnp.where(s_idx > sp, 1.0, 0.0)
                rank = rank + jnp.where(other > score, 1.0, jnp.where(other == score, tie, 0.0))
            sel_t = jnp.where(causal, jnp.where(rank < SLC_TOPN, 1.0, 0.0), 0.0)
        both = jnp.concatenate([sel_t, pad, sel_t, pad], axis=0).T
        sel_ref[0, 0, rows, :] = jnp.where(bias_lane, jnp.where(both > 0.5, 0.0, NEG_INF), 0.0).astype(BF16)


def _select(qc, kvc):
    b, t, _ = qc.shape
    n = t // CMP_STRIDE
    grp = pl.BlockSpec((1, t, GROUP_W), lambda i, g: (i, 0, g))
    return pl.pallas_call(
        _sel_kernel,
        grid=(b, NSA_GROUPS),
        in_specs=[grp, pl.BlockSpec((1, 1, n, LANES), lambda i, g: (i, g, 0, 0)),
                  pl.BlockSpec((1, 1, n, LANES), lambda i, g: (i, NSA_GROUPS + g, 0, 0))],
        out_specs=[pl.BlockSpec((1, 1, t, LANES), lambda i, g: (i, g, 0, 0)), grp],
        out_shape=[jax.ShapeDtypeStruct((b, NSA_GROUPS, t, LANES), BF16),
                   jax.ShapeDtypeStruct((b, t, NSA_W), F32)],
        compiler_params=_params("parallel", "parallel"),
        name="nsa_sel",
    )(qc, kvc, kvc)


def _swap_halves(x):
    return pltpu.roll(x.astype(F32), HEAD_DIM, 1).astype(x.dtype)


def _nsa_kernel(qr_ref, ocmp_ref, ksw_ref, vsw_ref, gate_ref, sel_ref, o_ref, vsa_ref, vwa_ref, kse_ref):
    t = qr_ref.shape[1]
    qb = QB_NSA
    group = pl.program_id(1)

    lane_t = lax.broadcasted_iota(jnp.int32, (t, LANES), 1)
    lo_t = lane_t < HEAD_DIM
    vsw = vsw_ref[0]
    vws = _swap_halves(vsw)
    _fill_value_slab(vsa_ref, jnp.where(lo_t, vsw, vws))
    _fill_value_slab(vwa_ref, jnp.where(lo_t, vws, vsw))
    key_blk = lax.shift_right_logical(lax.broadcasted_iota(jnp.int32, (t, LANES), 0), SLC_SHIFT)
    onehot = jnp.where((lane_t & (HEAD_DIM - 1)) == key_blk, 1.0, 0.0).astype(BF16)
    kse_ref[...] = jnp.where(lo_t, ksw_ref[0], onehot)

    lane = lax.broadcasted_iota(jnp.int32, (qb, LANES), 1)
    lo_lane = lane < HEAD_DIM
    diag_bias = _causal_bias(qb)

    def win_bias(w0, r0):
        n = r0 + qb - w0
        kpos = w0 + lax.broadcasted_iota(jnp.int32, (qb, n), 1)
        qpos = r0 + lax.broadcasted_iota(jnp.int32, (qb, n), 0)
        return jnp.where(kpos <= qpos, jnp.where(kpos > qpos - WIN, 0.0, NEG_INF), NEG_INF)

    steady_win_bias = win_bias(0, WIN)

    def normalised(oa):
        return oa[:, :LANES] / oa[:, LANES:]

    def scores(i, pp):
        r0 = i * qb
        rows = slice(r0, r0 + qb)
        qr = qr_ref[0, rows, pp * LANES:(pp + 1) * LANES]
        zero = jnp.zeros_like(qr)
        selb = sel_ref[0, 0, rows, :]
        q_sw = _swap_halves(qr)
        qs = jnp.concatenate([jnp.where(lo_lane, qr, selb), jnp.where(lo_lane, q_sw, selb)], axis=0)
        s_slc = _dot_nt(qs, kse_ref[:r0 + qb, :]).reshape(2, qb, r0 + qb)
        w0 = max(r0 - WIN, 0)
        q2 = jnp.concatenate([jnp.where(lo_lane, zero, q_sw), jnp.where(lo_lane, zero, qr)], axis=0)
        s_win = _dot_nt(q2, ksw_ref[0, w0:r0 + qb, :]).reshape(2, qb, r0 + qb - w0)
        return s_slc, s_win

    order = [(i, pp) for i in reversed(range(t // qb)) for pp in range(PAIRS_PER_STEP)]
    pending = [scores(*o) for o in order[:LOOKAHEAD]]
    for pos, (i, pp) in enumerate(order):
        pair = pl.program_id(2) * PAIRS_PER_STEP + pp
        cols = slice(pp * LANES, (pp + 1) * LANES)
        r0 = i * qb
        rows = slice(r0, r0 + qb)
        s_slc, s_win = pending.pop(0)
        if pos + LOOKAHEAD < len(order):
            pending.append(scores(*order[pos + LOOKAHEAD]))
        p = _causal_probs(s_slc, diag_bias)
        o_slc = normalised(_dot(p.reshape(2 * qb, r0 + qb), vsa_ref[:r0 + qb, :]))
        w0 = max(r0 - WIN, 0)
        sw = s_win + (steady_win_bias if r0 >= WIN else win_bias(w0, r0))[None]
        pw = jnp.exp2(sw - jnp.max(sw, axis=2, keepdims=True)).astype(BF16).reshape(2 * qb, r0 + qb - w0)
        o_win = normalised(_dot(pw, vwa_ref[w0:r0 + qb, :]))

        gate = gate_ref[0, rows, :]
        o_cmp = ocmp_ref[0, rows, cols]
        outs = []
        for par in range(2):
            col = GATE_PER_GROUP * group + 6 * pair + 3 * par
            g = [jnp.sum(jnp.where(lane == col + br, gate, 0.0), axis=1, keepdims=True) for br in range(3)]
            hr = slice(par * qb, (par + 1) * qb)
            outs.append(g[0] * o_cmp + g[1] * o_slc[hr] + g[2] * o_win[hr])
        o_ref[0, rows, cols] = jnp.where(lo_lane, outs[0], outs[1]).astype(o_ref.dtype)


def _nsa(qr, ocmp, ksw, vsw, gns, selb):
    b, t, _ = qr.shape
    steps = PAIRS // PAIRS_PER_STEP
    pair = pl.BlockSpec((1, t, PAIRS_PER_STEP * LANES), lambda i, g, p: (i, 0, g * steps + p))
    grp = pl.BlockSpec((1, t, LANES), lambda i, g, p: (i, 0, g))
    gates = pl.BlockSpec((1, t, LANES), lambda i, g, p: (i, 0, 0))
    sel = pl.BlockSpec((1, 1, t, LANES), lambda i, g, p: (i, g, 0, 0))
    slab = pltpu.VMEM((t, 2 * LANES), BF16)
    keys = pltpu.VMEM((t, LANES), BF16)
    return pl.pallas_call(
        _nsa_kernel,
        grid=(b, NSA_GROUPS, steps),
        in_specs=[pair, pair, grp, grp, gates, sel],
        out_specs=pair,
        out_shape=jax.ShapeDtypeStruct((b, t, NSA_W), BF16),
        scratch_shapes=[slab, slab, keys],
        compiler_params=_params("parallel", "parallel", "parallel"),
        name="nsa_attn",
    )(qr, ocmp, ksw, vsw, gns, selb)


def _merge_ffn_kernel(x_ref, yda_ref, yns_ref, ga_ref, gb_ref, wa_ref, wb_ref, wo_ref,
                      g_ref, w1_ref, w3_ref, w2_ref, fg_ref, o_ref):
    merged = ga_ref[...] * _dot(yda_ref[...], wa_ref[...]) + gb_ref[...] * _dot(yns_ref[...], wb_ref[...])
    x = x_ref[...] + _dot(merged.astype(BF16), wo_ref[...])
    o_ref[...] = _rms(_swiglu_residual(x, g_ref[...], w1_ref, w3_ref, w2_ref), fg_ref[...])


def _merge_ffn(x2, yda, yns, ga, gb, w_da, w_nsa, w_out, g, w1, w3, w2, final_g):
    n = x2.shape[0]
    row = pl.BlockSpec((TM_FFN, D_MODEL), lambda i: (i, 0))
    wspec = _resident((D_MODEL, D_MODEL))
    return pl.pallas_call(
        _merge_ffn_kernel,
        grid=(n // TM_FFN,),
        in_specs=[row] * 5 + [wspec] * 3 + _ffn_specs() + [_resident((1, D_MODEL))],
        out_specs=row,
        out_shape=jax.ShapeDtypeStruct((n, D_MODEL), F32),
        compiler_params=_params("parallel"),
        name="merge_ffn",
    )(x2, yda, yns, ga, gb, w_da.astype(BF16), w_nsa.astype(BF16), w_out.astype(BF16),
      g.reshape(1, D_MODEL), w1.astype(BF16), w3.astype(BF16), w2.astype(BF16), final_g.reshape(1, D_MODEL))


def kernel(x, ffn1_norm, ffn1_w1, ffn1_w3, ffn1_w2, mix_norm, w_in, da_lambda, da_head_norm, cmp_pos, cmp_w1,
           cmp_w2, w_proj_da, w_proj_nsa, w_out, ffn2_norm, ffn2_w1, ffn2_w3, ffn2_w2, final_norm):
    b, t, d = x.shape
    depth = ffn1_norm.shape[0]
    assert depth == 1, "one decoder layer per call"
    lam_init = 0.8 - 0.6 * math.exp(-0.3 * 0)
    (x1, qda, kda, vda, qnr, qnc, kcr, vcr, ksw, vsw, gns, ga, gb) = _ffn_inproj(
        x, ffn1_norm[0], ffn1_w1[0], ffn1_w3[0], ffn1_w2[0], mix_norm[0], w_in[0])
    x2 = x1.reshape(b * t, d)
    yda = _diff_attention(qda, kda, vda, da_lambda[0], da_head_norm[0], lam_init)
    kvc = _compress(kcr, vcr, cmp_pos[0], cmp_w1[0], cmp_w2[0])
    selb, ocmp = _select(qnc, kvc)
    yns = _nsa(qnr, ocmp, ksw, vsw, gns, selb)
    flat = lambda a: a.reshape(b * t, a.shape[-1])
    out = _merge_ffn(x2, flat(yda), flat(yns), flat(ga), flat(gb), w_proj_da[0], w_proj_nsa[0], w_out[0],
                     ffn2_norm[0], ffn2_w1[0], ffn2_w3[0], ffn2_w2[0], final_norm)
    return out.reshape(b, t, d)
```
